```python
import math
import jax, jax.numpy as jnp
from jax import lax
import numpy as np

D_MODEL = 2048
BATCH = 2
SEQ = 16384
DEPTH = 1
DEC_BATCH = 4
DEC_SEQ = 2048
PAST_LEN = 128

H_A = 8
HD_A = 64
DIL_CONFIGS = ((128, 1), (512, 4), (2048, 16))
N_GROUPS_B = 3
H_B = 4
HD_B = 128
D_FF = 5632
CONV_W = 3
Q_BLOCK = 128
NORM_EPS = 1e-6
SUBLN_EPS = 1e-5
NEG_INF = -1e30

QA_W = H_A * 2 * HD_A
VA_W = H_A * 2 * HD_A
QB_W = N_GROUPS_B * H_B * HD_B
OA_W = VA_W
OB_W = H_B * HD_B
SPLIT_SIZES = [QA_W, QA_W, VA_W, QB_W, QB_W, QB_W, D_MODEL, D_MODEL]
SPLIT_IDX = [int(v) for v in np.cumsum(SPLIT_SIZES)[:-1]]
N_IN = int(sum(SPLIT_SIZES))

kernel_name = 'hybrid_diff_dilated_encoder'


def alibi_slopes(n):
    return jnp.asarray(2.0 ** (-8.0 * (np.arange(n) + 1) / n), dtype=jnp.float32)


def rmsnorm(x, g, eps):
    xf = x.astype(jnp.float32)
    y = xf * lax.rsqrt(jnp.mean(xf * xf, axis=-1, keepdims=True) + eps)
    return (y * g.astype(jnp.float32)).astype(x.dtype)


def diff_attention(q, k, v, lam, slopes):
    B, S = q.shape[:2]
    nb = S // Q_BLOCK
    scale = HD_A ** -0.5
    kpos = jnp.arange(S, dtype=jnp.float32)
    qblocks = q.reshape(B, nb, Q_BLOCK, H_A, 2, HD_A).swapaxes(0, 1)
    starts = jnp.arange(nb, dtype=jnp.float32) * Q_BLOCK

    def one_block(args):
        qb, start = args
        s = jnp.einsum('bqhcd,bkhcd->bhcqk', qb, k, preferred_element_type=jnp.float32) * scale
        qpos = start + jnp.arange(Q_BLOCK, dtype=jnp.float32)
        dist = jnp.abs(qpos[:, None] - kpos[None, :])
        s = s - slopes[None, :, None, None, None] * dist
        p = jax.nn.softmax(s, axis=-1)
        w = p[:, :, 0] - lam * p[:, :, 1]
        return jnp.einsum('bhqk,bkhe->bqhe', w.astype(v.dtype), v)

    out = lax.map(one_block, (qblocks, starts))
    return out.swapaxes(0, 1).reshape(B, S, H_A, 2 * HD_A)


def to_strided(x, r):
    B, S = x.shape[:2]
    rest = x.shape[2:]
    return x.reshape((B, S // r, r) + rest).swapaxes(1, 2).reshape((B * r, S // r) + rest)


def from_strided(x, r, B):
    N, L = x.shape[:2]
    rest = x.shape[2:]
    return x.reshape((B, r, L) + rest).swapaxes(1, 2).reshape((B, L * r) + rest)


def banded_attention(q, k, v, slopes, dil, half):
    N, L, H, D = q.shape
    qb = min(Q_BLOCK, L)
    nb = -(-L // qb)
    Lp = nb * qb
    kw = qb + 2 * half
    scale = D ** -0.5
    qp = jnp.pad(q, ((0, 0), (0, Lp - L), (0, 0), (0, 0))).reshape(N, nb, qb, H, D)
    pad_k = ((0, 0), (half, Lp - L + half), (0, 0), (0, 0))
    kp = jnp.pad(k, pad_k)
    vp = jnp.pad(v, pad_k)
    kidx = jnp.arange(nb)[:, None] * qb + jnp.arange(kw)[None, :]
    kb = kp[:, kidx]
    vb = vp[:, kidx]
    s = jnp.einsum('nbqhd,nbkhd->nbhqk', qp, kb, preferred_element_type=jnp.float32) * scale
    qpos = jnp.arange(nb)[:, None] * qb + jnp.arange(qb)[None, :]
    kpos = kidx - half
    rel = kpos[:, None, :] - qpos[:, :, None]
    valid = (jnp.abs(rel) <= half) & (kpos[:, None, :] >= 0) & (kpos[:, None, :] < L)
    dist = (dil * jnp.abs(rel)).astype(jnp.float32)
    bias = -(slopes[:, None, None, None] * dist[None]).transpose(1, 0, 2, 3)[None]
    s = jnp.where(valid[None, :, None], s + bias, NEG_INF)
    lse = jax.nn.logsumexp(s, axis=-1)
    p = jnp.exp(s - lse[..., None])
    o = jnp.einsum('nbhqk,nbkhd->nbqhd', p.astype(v.dtype), vb).reshape(N, Lp, H, D)[:, :L]
    lse = lse.transpose(0, 1, 3, 2).reshape(N, Lp, H)[:, :L]
    return o, lse


def dilated_mixer(q, k, v, slopes_b):
    B, S = q.shape[:2]
    outs, lses = [], []
    for g, (window, dil) in enumerate(DIL_CONFIGS):
        half = (window // 2) // dil
        o, lse = banded_attention(to_strided(q[:, :, g], dil), to_strided(k[:, :, g], dil),
                                  to_strided(v[:, :, g], dil), slopes_b[g], dil, half)
        outs.append(from_strided(o, dil, B))
        lses.append(from_strided(lse, dil, B))
    alpha = jax.nn.softmax(jnp.stack(lses), axis=0)
    out = jnp.einsum('gbsh,gbshd->bshd', alpha, jnp.stack(outs).astype(jnp.float32))
    return out.astype(q.dtype).reshape(B, S, OB_W)


def dwconv_centred(u, w, b):
    up = jnp.pad(u, ((0, 0), (1, 1), (0, 0)))
    return up[:, :-2] * w[0] + up[:, 1:-1] * w[1] + up[:, 2:] * w[2] + b


def encoder_layer(x, layer_idx, g_mix_norm, w_in, g_qa, g_ka, lam_q1, lam_k1, lam_q2, lam_k2,
                  g_subln, g_qb, g_kb, w_pa, w_pb, w_o, g_ffn_norm, w_up, conv_w, conv_b, w_down):
    B, S, _ = x.shape
    h = rmsnorm(x, g_mix_norm, NORM_EPS)
    z = h @ w_in
    qa, ka, va, qb, kb, vb, gate_a, gate_b = jnp.split(z, SPLIT_IDX, axis=-1)

    qa = rmsnorm(qa.reshape(B, S, H_A, 2, HD_A), g_qa, NORM_EPS)
    ka = rmsnorm(ka.reshape(B, S, H_A, 2, HD_A), g_ka, NORM_EPS)
    va = va.reshape(B, S, H_A, 2 * HD_A)
    lam_init = 0.8 - 0.6 * math.exp(-0.3 * layer_idx)
    lam = (jnp.exp(jnp.sum(lam_q1.astype(jnp.float32) * lam_k1.astype(jnp.float32)))
           - jnp.exp(jnp.sum(lam_q2.astype(jnp.float32) * lam_k2.astype(jnp.float32))) + lam_init)
    oa = diff_attention(qa, ka, va, lam, alibi_slopes(H_A))
    oa = (rmsnorm(oa, g_subln, SUBLN_EPS) * (1.0 - lam_init)).reshape(B, S, OA_W)

    qb = rmsnorm(qb.reshape(B, S, N_GROUPS_B, H_B, HD_B), g_qb, NORM_EPS)
    kb = rmsnorm(kb.reshape(B, S, N_GROUPS_B, H_B, HD_B), g_kb, NORM_EPS)
    vb = vb.reshape(B, S, N_GROUPS_B, H_B, HD_B)
    ob = dilated_mixer(qb, kb, vb, alibi_slopes(N_GROUPS_B * H_B).reshape(N_GROUPS_B, H_B))

    merged = jax.nn.sigmoid(gate_a) * (oa @ w_pa) + jax.nn.sigmoid(gate_b) * (ob @ w_pb)
    x = x + merged @ w_o

    h2 = rmsnorm(x, g_ffn_norm, NORM_EPS)
    u = dwconv_centred(h2 @ w_up, conv_w, conv_b)
    ug, uv = jnp.split(u, [D_FF], axis=-1)
    return x + (jax.nn.gelu(ug, approximate=False) * uv) @ w_down


def setup_inputs(seed: int = 0) -> dict:
    key = jax.random.key(seed)
    ks = jax.random.split(key, 24)
    f32 = jnp.float32
    nrm = lambda k, shape, s: jax.random.normal(k, shape, f32) * s
    gain = lambda k, shape: 1.0 + 0.02 * jax.random.normal(k, shape, f32)
    return {
        'x_prompt': jax.random.normal(ks[0], (BATCH, SEQ, D_MODEL), f32),
        'x_sample': jax.random.normal(ks[1], (DEC_BATCH, DEC_SEQ, D_MODEL), f32),
        'g_mix_norm': gain(ks[2], (DEPTH, D_MODEL)),
        'w_in': nrm(ks[3], (DEPTH, D_MODEL, N_IN), D_MODEL ** -0.5),
        'g_qa': gain(ks[4], (DEPTH, HD_A)),
        'g_ka': gain(ks[5], (DEPTH, HD_A)),
        'lam_q1': nrm(ks[6], (DEPTH, HD_A), 0.1),
        'lam_k1': nrm(ks[7], (DEPTH, HD_A), 0.1),
        'lam_q2': nrm(ks[8], (DEPTH, HD_A), 0.1),
        'lam_k2': nrm(ks[9], (DEPTH, HD_A), 0.1),
        'g_subln': gain(ks[10], (DEPTH, 2 * HD_A)),
        'g_qb': gain(ks[11], (DEPTH, HD_B)),
        'g_kb': gain(ks[12], (DEPTH, HD_B)),
        'w_pa': nrm(ks[13], (DEPTH, OA_W, D_MODEL), OA_W ** -0.5),
        'w_pb': nrm(ks[14], (DEPTH, OB_W, D_MODEL), OB_W ** -0.5),
        'w_o': nrm(ks[15], (DEPTH, D_MODEL, D_MODEL), D_MODEL ** -0.5),
        'g_ffn_norm': gain(ks[16], (DEPTH, D_MODEL)),
        'w_up': nrm(ks[17], (DEPTH, D_MODEL, 2 * D_FF), D_MODEL ** -0.5),
        'conv_w': nrm(ks[18], (DEPTH, CONV_W, 2 * D_FF), CONV_W ** -0.5),
        'conv_b': nrm(ks[19], (DEPTH, 2 * D_FF), 0.01),
        'w_down': nrm(ks[20], (DEPTH, D_FF, D_MODEL), D_FF ** -0.5),
    }


def reference(x_prompt, x_sample, g_mix_norm, w_in, g_qa, g_ka, lam_q1, lam_k1, lam_q2, lam_k2,
              g_subln, g_qb, g_kb, w_pa, w_pb, w_o, g_ffn_norm, w_up, conv_w, conv_b, w_down):
    y_prompt = x_prompt
    y_sample = x_sample
    for l in range(DEPTH):
        p = (g_mix_norm[l], w_in[l], g_qa[l], g_ka[l], lam_q1[l], lam_k1[l], lam_q2[l], lam_k2[l],
             g_subln[l], g_qb[l], g_kb[l], w_pa[l], w_pb[l], w_o[l], g_ffn_norm[l], w_up[l],
             conv_w[l], conv_b[l], w_down[l])
        y_prompt = encoder_layer(y_prompt, l, *p)
        y_sample = encoder_layer(y_sample, l, *p)
    return (y_prompt, y_sample)
```

```python
import functools
import math

import numpy as np
import jax
import jax.numpy as jnp
from jax import lax
from jax.experimental import pallas as pl
from jax.experimental.pallas import tpu as pltpu

D_MODEL = 2048
H_A = 8
HD_A = 64
DIL_CONFIGS = ((128, 1), (512, 4), (2048, 16))
N_GROUPS_B = 3
H_B = 4
HD_B = 128
D_FF = 5632
Q_BLOCK = 128
NORM_EPS = 1e-6
SUBLN_EPS = 1e-5
NEG_INF = -1e30

QA_W = H_A * 2 * HD_A
QB_W = N_GROUPS_B * H_B * HD_B
OA_W = QA_W
OB_W = H_B * HD_B
N_IN = 3 * QA_W + 3 * QB_W + 2 * D_MODEL

LANES = 128
COL_GATE_A = 0
COL_GATE_B = D_MODEL
COL_QA = 2 * D_MODEL
COL_KA = COL_QA + QA_W
COL_VA = COL_KA + QA_W
COL_QB = COL_VA + QA_W
COL_KB = COL_QB + QB_W
COL_VB = COL_KB + QB_W
TN = 512
N_TILES = N_IN // TN
VMEM_LIMIT = 56 * 1024 * 1024

F32 = jnp.float32
BF16 = jnp.bfloat16


def _alibi_slopes(n):
    return np.asarray(2.0 ** (-8.0 * (np.arange(n) + 1) / n), dtype=np.float32)


def _in_proj_kernel(x_ref, gmix_ref, w_ref, gvec_ref, gmat_ref, z_ref, h_scr):
    j = pl.program_id(1)

    @pl.when(j == 0)
    def _():
        x = x_ref[...]
        ms = jnp.mean(x * x, axis=-1, keepdims=True)
        h_scr[...] = (x * lax.rsqrt(ms + NORM_EPS) * gmix_ref[...]).astype(BF16)

    z = jnp.dot(h_scr[...], w_ref[...], preferred_element_type=F32)
    t_qa, t_va, t_qb, t_vb = COL_QA // TN, COL_VA // TN, COL_QB // TN, COL_VB // TN
    is_gate = j < t_qa
    is_norm = ((j >= t_qa) & (j < t_va)) | ((j >= t_qb) & (j < t_vb))

    @pl.when(is_gate)
    def _():
        z_ref[...] = jax.nn.sigmoid(z).astype(BF16)

    @pl.when(is_norm)
    def _():
        ms = jnp.dot((z * z).astype(BF16), gmat_ref[...], preferred_element_type=F32)
        z_ref[...] = (z * lax.rsqrt(ms + NORM_EPS) * gvec_ref[...]).astype(BF16)

    @pl.when(jnp.logical_not(is_gate | is_norm))
    def _():
        z_ref[...] = z.astype(BF16)


def _in_proj(x2, gmix, w_in_p, gvec, gmat, tm):
    T = x2.shape[0]
    t_qb = COL_QB // TN
    return pl.pallas_call(
        _in_proj_kernel,
        grid=(T // tm, N_TILES),
        in_specs=[
            pl.BlockSpec((tm, D_MODEL), lambda i, j: (i, 0)),
            pl.BlockSpec((1, D_MODEL), lambda i, j: (0, 0)),
            pl.BlockSpec((D_MODEL, TN), lambda i, j: (0, j)),
            pl.BlockSpec((1, TN), lambda i, j: (0, j)),
            pl.BlockSpec((None, TN, TN), lambda i, j: (jnp.where(j < t_qb, 0, 1), 0, 0)),
        ],
        out_specs=pl.BlockSpec((tm, TN), lambda i, j: (i, j)),
        out_shape=jax.ShapeDtypeStruct((T, N_IN), BF16),
        scratch_shapes=[pltpu.VMEM((tm, D_MODEL), BF16)],
        compiler_params=pltpu.CompilerParams(
            dimension_semantics=("parallel", "arbitrary"), vmem_limit_bytes=VMEM_LIMIT),
        name="in_proj",
    )(x2, gmix, w_in_p, gvec, gmat)


def _diff_attn_kernel(slopes_ref, lamv_ref, q_ref, k_ref, v_ref, gsub_ref, o_ref,
                      m_scr, l_scr, acc_scr, *, tq, tk, seq, lam_init):
    h = pl.program_id(1)
    i = pl.program_id(2)
    slope = slopes_ref[h]
    q = q_ref[...]
    lane = lax.broadcasted_iota(jnp.int32, (tq, LANES), 1)
    zero = jnp.zeros_like(q)
    qq = jnp.concatenate([jnp.where(lane < HD_A, q, zero), jnp.where(lane >= HD_A, q, zero)], axis=0)
    row = lax.broadcasted_iota(jnp.int32, (2 * tq, tk), 0)
    col = lax.broadcasted_iota(jnp.int32, (2 * tq, tk), 1)
    rel = (jnp.where(row >= tq, row - tq, row) - col).astype(F32)

    m_scr[...] = jnp.full(m_scr.shape, NEG_INF, F32)
    l_scr[...] = jnp.zeros(l_scr.shape, F32)
    acc_scr[...] = jnp.zeros(acc_scr.shape, F32)

    def body(c, carry):
        start = pl.multiple_of(c * tk, tk)
        kc = k_ref[pl.ds(start, tk), :]
        vc = v_ref[pl.ds(start, tk), :]
        s = lax.dot_general(qq, kc, (((1,), (1,)), ((), ())), preferred_element_type=F32)
        delta = (i * tq - c * tk).astype(F32)
        s = s - slope * jnp.abs(rel + delta)
        m_prev = m_scr[...]
        m_new = jnp.maximum(m_prev, jnp.max(s, axis=-1, keepdims=True))
        p = jnp.exp(s - m_new)
        alpha = jnp.exp(m_prev - m_new)
        l_scr[...] = alpha * l_scr[...] + jnp.sum(p, axis=-1, keepdims=True)
        acc_scr[...] = alpha * acc_scr[...] + jnp.dot(p.astype(BF16), vc, preferred_element_type=F32)
        m_scr[...] = m_new
        return carry

    lax.fori_loop(0, seq // tk, body, 0)

    lamv = lamv_ref[...]
    lam = (jnp.exp(jnp.sum(lamv[0:1] * lamv[1:2], axis=-1, keepdims=True))
           - jnp.exp(jnp.sum(lamv[2:3] * lamv[3:4], axis=-1, keepdims=True)) + lam_init)
    o = acc_scr[...] / l_scr[...]
    d = o[:tq] - lam * o[tq:]
    ms = jnp.mean(d * d, axis=-1, keepdims=True)
    o_ref[...] = (d * lax.rsqrt(ms + SUBLN_EPS) * gsub_ref[...] * (1.0 - lam_init)).astype(BF16)


def _diff_attn(z3, slopes_a, lamv, gsub, lam_init, tq, tk):
    B, S, _ = z3.shape
    qb0, kb0, vb0 = COL_QA // LANES, COL_KA // LANES, COL_VA // LANES
    kernel = functools.partial(_diff_attn_kernel, tq=tq, tk=tk, seq=S, lam_init=lam_init)
    return pl.pallas_call(
        kernel,
        grid=(B, H_A, S // tq),
        in_specs=[
            pl.BlockSpec(memory_space=pltpu.SMEM),
            pl.BlockSpec((4, HD_A), lambda b, h, i: (0, 0)),
            pl.BlockSpec((None, tq, LANES), lambda b, h, i: (b, i, qb0 + h)),
            pl.BlockSpec((None, S, LANES), lambda b, h, i: (b, 0, kb0 + h)),
            pl.BlockSpec((None, S, LANES), lambda b, h, i: (b, 0, vb0 + h)),
            pl.BlockSpec((1, LANES), lambda b, h, i: (0, 0)),
        ],
        out_specs=pl.BlockSpec((None, tq, LANES), lambda b, h, i: (b, i, h)),
        out_shape=jax.ShapeDtypeStruct((B, S, OA_W), BF16),
        scratch_shapes=[pltpu.VMEM((2 * tq, 1), F32), pltpu.VMEM((2 * tq, 1), F32),
                        pltpu.VMEM((2 * tq, LANES), F32)],
        compiler_params=pltpu.CompilerParams(
            dimension_semantics=("parallel", "parallel", "arbitrary"), vmem_limit_bytes=VMEM_LIMIT),
        name="diff_attn",
    )(slopes_a, lamv, z3, z3, z3, gsub)


def _band_attn_kernel(q_ref, kp_ref, kc_ref, kn_ref, vp_ref, vc_ref, vn_ref, o_ref, lse_ref,
                      *, length, dil, half, slopes):
    i = pl.program_id(2)
    qb = Q_BLOCK
    kw = 3 * qb
    row = lax.broadcasted_iota(jnp.int32, (qb, kw), 0)
    col = lax.broadcasted_iota(jnp.int32, (qb, kw), 1)
    rel = col - qb - row
    kpos = (i - 1) * qb + col
    valid = (jnp.abs(rel) <= half) & (kpos >= 0) & (kpos < length)
    dist = (dil * jnp.abs(rel)).astype(F32)
    for hh in range(H_B):
        sl = slice(hh * HD_B, (hh + 1) * HD_B)
        kk = jnp.concatenate([kp_ref[:, sl], kc_ref[:, sl], kn_ref[:, sl]], axis=0)
        vv = jnp.concatenate([vp_ref[:, sl], vc_ref[:, sl], vn_ref[:, sl]], axis=0)
        s = lax.dot_general(q_ref[:, sl], kk, (((1,), (1,)), ((), ())), preferred_element_type=F32)
        s = jnp.where(valid, s - float(slopes[hh]) * dist, NEG_INF)
        m = jnp.max(s, axis=-1, keepdims=True)
        p = jnp.exp(s - m)
        l = jnp.sum(p, axis=-1, keepdims=True)
        o = jnp.dot(p.astype(BF16), vv, preferred_element_type=F32) / l
        o_ref[:, sl] = o.astype(o_ref.dtype)
        lse_ref[:, sl] = jnp.broadcast_to(m + jnp.log(l), (qb, HD_B))


def _band_attn(z3, g, window, dil):
    B, S, _ = z3.shape
    L = S // dil
    half = (window // 2) // dil
    nb = L // Q_BLOCK
    zs = z3.reshape(B, L, dil * N_IN)
    per_tok = N_IN // TN
    qo, ko, vo = COL_QB // TN + g, COL_KB // TN + g, COL_VB // TN + g
    slopes = _alibi_slopes(N_GROUPS_B * H_B).reshape(N_GROUPS_B, H_B)[g]
    kernel = functools.partial(_band_attn_kernel, length=L, dil=dil, half=half, slopes=slopes)

    def spec(off, shift):
        def imap(b, c, i):
            return (b, jnp.clip(i + shift, 0, nb - 1), c * per_tok + off)
        return pl.BlockSpec((None, Q_BLOCK, TN), imap)

    out_spec = pl.BlockSpec((None, Q_BLOCK, OB_W), lambda b, c, i: (b, i, c))
    o, lse = pl.pallas_call(
        kernel,
        grid=(B, dil, nb),
        in_specs=[spec(qo, 0), spec(ko, -1), spec(ko, 0), spec(ko, 1),
                  spec(vo, -1), spec(vo, 0), spec(vo, 1)],
        out_specs=[out_spec, out_spec],
        out_shape=[jax.ShapeDtypeStruct((B, L, dil * OB_W), BF16),
                   jax.ShapeDtypeStruct((B, L, dil * OB_W), F32)],
        compiler_params=pltpu.CompilerParams(
            dimension_semantics=("parallel", "parallel", "arbitrary"), vmem_limit_bytes=VMEM_LIMIT),
        name=f"band_attn_g{g}",
    )(zs, zs, zs, zs, zs, zs, zs)
    return o.reshape(B * S, OB_W), lse.reshape(B * S, OB_W)


def _merge_kernel(x_ref, ga_ref, gb_ref, oa_ref, o0_ref, o1_ref, o2_ref, l0_ref, l1_ref, l2_ref,
                  wpa_ref, wpb_ref, wo_ref, gffn_ref, x1_ref, h2_ref):
    l0, l1, l2 = l0_ref[...], l1_ref[...], l2_ref[...]
    m = jnp.maximum(jnp.maximum(l0, l1), l2)
    w0, w1, w2 = jnp.exp(l0 - m), jnp.exp(l1 - m), jnp.exp(l2 - m)
    ob = (w0 * o0_ref[...].astype(F32) + w1 * o1_ref[...].astype(F32)
          + w2 * o2_ref[...].astype(F32)) / (w0 + w1 + w2)
    pa = jnp.dot(oa_ref[...], wpa_ref[...], preferred_element_type=F32)
    pb = jnp.dot(ob.astype(BF16), wpb_ref[...], preferred_element_type=F32)
    merged = ga_ref[...].astype(F32) * pa + gb_ref[...].astype(F32) * pb
    x1 = x_ref[...] + jnp.dot(merged.astype(BF16), wo_ref[...], preferred_element_type=F32)
    x1_ref[...] = x1
    ms = jnp.mean(x1 * x1, axis=-1, keepdims=True)
    h2_ref[...] = (x1 * lax.rsqrt(ms + NORM_EPS) * gffn_ref[...]).astype(BF16)


def _merge(x2, z2, oa2, obs, lses, wpa, wpb, wo, gffn, tm):
    T = x2.shape[0]
    row = lambda w: pl.BlockSpec((tm, w), lambda i: (i, 0))
    const = lambda a: pl.BlockSpec(a.shape, lambda i: (0, 0), pipeline_mode=pl.Buffered(1))
    return pl.pallas_call(
        _merge_kernel,
        grid=(T // tm,),
        in_specs=[row(D_MODEL),
                  pl.BlockSpec((tm, D_MODEL), lambda i: (i, COL_GATE_A // D_MODEL)),
                  pl.BlockSpec((tm, D_MODEL), lambda i: (i, COL_GATE_B // D_MODEL)),
                  row(OA_W), row(OB_W), row(OB_W), row(OB_W), row(OB_W), row(OB_W), row(OB_W),
                  const(wpa), const(wpb), const(wo), const(gffn)],
        out_specs=[row(D_MODEL), row(D_MODEL)],
        out_shape=[jax.ShapeDtypeStruct((T, D_MODEL), F32), jax.ShapeDtypeStruct((T, D_MODEL), BF16)],
        compiler_params=pltpu.CompilerParams(
            dimension_semantics=("parallel",), vmem_limit_bytes=VMEM_LIMIT),
        name="merge",
    )(x2, z2, z2, oa2, *obs, *lses, wpa, wpb, wo, gffn)


HALO = 16


def _ffn_kernel(h_ref, hp_ref, hn_ref, x1_ref, wg_ref, wv_ref, cwg_ref, cwv_ref, cbg_ref, cbv_ref,
                wd_ref, o_ref, hcat_scr, acc_scr, *, tm, tiles_per_seq, n_f):
    i = pl.program_id(0)
    f = pl.program_id(1)

    @pl.when(f == 0)
    def _():
        pos = i % tiles_per_seq
        hp = hp_ref[...]
        hn = hn_ref[...]
        hcat_scr[0:HALO, :] = jnp.where(pos == 0, jnp.zeros_like(hp), hp)
        hcat_scr[HALO:HALO + tm, :] = h_ref[...]
        hcat_scr[HALO + tm:, :] = jnp.where(pos == tiles_per_seq - 1, jnp.zeros_like(hn), hn)
        acc_scr[...] = jnp.zeros(acc_scr.shape, F32)

    hcat = hcat_scr[...]
    rows = tm + 2 * HALO

    def conv_half(w_ref, cw_ref, cb_ref):
        a = jnp.dot(hcat, w_ref[...], preferred_element_type=F32)
        a_prev = pltpu.roll(a, 1, 0)[HALO:HALO + tm]
        a_next = pltpu.roll(a, rows - 1, 0)[HALO:HALO + tm]
        cw = cw_ref[...]
        return a_prev * cw[0:1] + a[HALO:HALO + tm] * cw[1:2] + a_next * cw[2:3] + cb_ref[...]

    ug = conv_half(wg_ref, cwg_ref, cbg_ref)
    uv = conv_half(wv_ref, cwv_ref, cbv_ref)
    act = 0.5 * ug * (1.0 + lax.erf(ug * np.float32(math.sqrt(0.5)))) * uv
    acc_scr[...] += jnp.dot(act.astype(BF16), wd_ref[...], preferred_element_type=F32)

    @pl.when(f == n_f - 1)
    def _():
        o_ref[...] = x1_ref[...] + acc_scr[...]


def _ffn(h2, x1, w_up, conv_w, conv_b, w_down, seq, tm, tf):
    T = h2.shape[0]
    n_f = D_FF // tf
    hb = tm // HALO
    n_hb = T // HALO
    kernel = functools.partial(_ffn_kernel, tm=tm, tiles_per_seq=seq // tm, n_f=n_f)
    return pl.pallas_call(
        kernel,
        grid=(T // tm, n_f),
        in_specs=[
            pl.BlockSpec((tm, D_MODEL), lambda i, f: (i, 0)),
            pl.BlockSpec((HALO, D_MODEL), lambda i, f: (jnp.maximum(i * hb - 1, 0), 0)),
            pl.BlockSpec((HALO, D_MODEL), lambda i, f: (jnp.minimum((i + 1) * hb, n_hb - 1), 0)),
            pl.BlockSpec((tm, D_MODEL), lambda i, f: (i, 0)),
            pl.BlockSpec((D_MODEL, tf), lambda i, f: (0, f)),
            pl.BlockSpec((D_MODEL, tf), lambda i, f: (0, f + n_f)),
            pl.BlockSpec((3, tf), lambda i, f: (0, f)),
            pl.BlockSpec((3, tf), lambda i, f: (0, f + n_f)),
            pl.BlockSpec((1, tf), lambda i, f: (0, f)),
            pl.BlockSpec((1, tf), lambda i, f: (0, f + n_f)),
            pl.BlockSpec((tf, D_MODEL), lambda i, f: (f, 0)),
        ],
        out_specs=pl.BlockSpec((tm, D_MODEL), lambda i, f: (i, 0)),
        out_shape=jax.ShapeDtypeStruct((T, D_MODEL), F32),
        scratch_shapes=[pltpu.VMEM((tm + 2 * HALO, D_MODEL), BF16), pltpu.VMEM((tm, D_MODEL), F32)],
        compiler_params=pltpu.CompilerParams(
            dimension_semantics=("parallel", "arbitrary"), vmem_limit_bytes=VMEM_LIMIT),
        name="ffn",
    )(h2, h2, h2, x1, w_up, w_up, conv_w, conv_w, conv_b, conv_b, w_down)


def _prep_params(g_mix_norm, w_in, g_qa, g_ka, lam_q1, lam_k1, lam_q2, lam_k2, g_subln, g_qb, g_kb,
                 w_pa, w_pb, w_o, g_ffn_norm, w_up, conv_w, conv_b, w_down):
    qa, ka, va, qb, kb, vb, gate_a, gate_b = jnp.split(
        w_in, list(np.cumsum([QA_W, QA_W, QA_W, QB_W, QB_W, QB_W, D_MODEL])), axis=-1)
    w_in_p = jnp.concatenate([gate_a, gate_b, qa, ka, va, qb, kb, vb], axis=-1).astype(BF16)
    ones = lambda n: jnp.ones((n,), F32)
    gvec = jnp.concatenate([
        ones(2 * D_MODEL),
        jnp.tile(g_qa.astype(F32), 2 * H_A) * (HD_A ** -0.5),
        jnp.tile(g_ka.astype(F32), 2 * H_A),
        ones(QA_W),
        jnp.tile(g_qb.astype(F32), N_GROUPS_B * H_B) * (HD_B ** -0.5),
        jnp.tile(g_kb.astype(F32), N_GROUPS_B * H_B),
        ones(QB_W)]).reshape(1, N_IN)
    idx = np.arange(TN)
    gmat = jnp.asarray(np.stack([
        (idx[:, None] // HD_A == idx[None, :] // HD_A) / HD_A,
        (idx[:, None] // HD_B == idx[None, :] // HD_B) / HD_B]).astype(np.float32), BF16)
    lamv = jnp.stack([lam_q1, lam_k1, lam_q2, lam_k2]).astype(F32)
    return dict(
        gmix=g_mix_norm.astype(F32).reshape(1, D_MODEL), w_in=w_in_p, gvec=gvec, gmat=gmat, lamv=lamv,
        gsub=g_subln.astype(F32).reshape(1, 2 * HD_A), wpa=w_pa.astype(BF16), wpb=w_pb.astype(BF16),
        wo=w_o.astype(BF16), gffn=g_ffn_norm.astype(F32).reshape(1, D_MODEL), w_up=w_up.astype(BF16),
        conv_w=conv_w.astype(F32), conv_b=conv_b.astype(F32).reshape(1, 2 * D_FF),
        w_down=w_down.astype(BF16))


def _encoder_layer(x, layer_idx, p):
    B, S, _ = x.shape
    T = B * S
    lam_init = 0.8 - 0.6 * math.exp(-0.3 * layer_idx)
    x2 = x.reshape(T, D_MODEL)
    z2 = _in_proj(x2, p["gmix"], p["w_in"], p["gvec"], p["gmat"], tm=min(1024, T))
    z3 = z2.reshape(B, S, N_IN)
    oa = _diff_attn(z3, jnp.asarray(_alibi_slopes(H_A)), p["lamv"], p["gsub"], lam_init,
                    tq=256, tk=512)
    obs, lses = [], []
    for g, (window, dil) in enumerate(DIL_CONFIGS):
        o, lse = _band_attn(z3, g, window, dil)
        obs.append(o)
        lses.append(lse)
    x1, h2 = _merge(x2, z2, oa.reshape(T, OA_W), obs, lses, p["wpa"], p["wpb"], p["wo"], p["gffn"],
                    tm=256)
    y = _ffn(h2, x1, p["w_up"], p["conv_w"], p["conv_b"], p["w_down"], seq=S, tm=512, tf=512)
    return y.reshape(B, S, D_MODEL)


def kernel(x_prompt, x_sample, g_mix_norm, w_in, g_qa, g_ka, lam_q1, lam_k1, lam_q2, lam_k2, g_subln,
           g_qb, g_kb, w_pa, w_pb, w_o, g_ffn_norm, w_up, conv_w, conv_b, w_down):
    y_prompt, y_sample = x_prompt, x_sample
    for l in range(w_in.shape[0]):
        p = _prep_params(g_mix_norm[l], w_in[l], g_qa[l], g_ka[l], lam_q1[l], lam_k1[l], lam_q2[l],
                         lam_k2[l], g_subln[l], g_qb[l], g_kb[l], w_pa[l], w_pb[l], w_o[l],
                         g_ffn_norm[l], w_up[l], conv_w[l], conv_b[l], w_down[l])
        y_prompt = _encoder_layer(y_prompt, l, p)
        y_sample = _encoder_layer(y_sample, l, p)
    return (y_prompt, y_sample)
```

```python
import functools
import math

import numpy as np
import jax
import jax.numpy as jnp
from jax import lax
from jax.experimental import pallas as pl
from jax.experimental.pallas import tpu as pltpu

D_MODEL = 2048
H_A = 8
HD_A = 64
DIL_CONFIGS = ((128, 1), (512, 4), (2048, 16))
N_GROUPS_B = 3
H_B = 4
HD_B = 128
D_FF = 5632
Q_BLOCK = 128
NORM_EPS = 1e-6
SUBLN_EPS = 1e-5
NEG_INF = -1e30

QA_W = H_A * 2 * HD_A
QB_W = N_GROUPS_B * H_B * HD_B
OA_W = QA_W
OB_W = H_B * HD_B
N_IN = 3 * QA_W + 3 * QB_W + 2 * D_MODEL

LANES = 128
COL_GATE_A = 0
COL_GATE_B = D_MODEL
COL_QA = 2 * D_MODEL
COL_KA = COL_QA + QA_W
COL_VA = COL_KA + QA_W
COL_QB = COL_VA + QA_W
COL_KB = COL_QB + QB_W
COL_VB = COL_KB + QB_W
TN = 512
N_TILES = N_IN // TN
VMEM_LIMIT = 56 * 1024 * 1024

F32 = jnp.float32
BF16 = jnp.bfloat16


def _alibi_slopes(n):
    return np.asarray(2.0 ** (-8.0 * (np.arange(n) + 1) / n), dtype=np.float32)


def _in_proj_kernel(x_ref, gmix_ref, w_ref, gvec_ref, gmat_ref, z_ref, h_scr):
    j = pl.program_id(1)

    @pl.when(j == 0)
    def _():
        x = x_ref[...]
        ms = jnp.mean(x * x, axis=-1, keepdims=True)
        h_scr[...] = (x * lax.rsqrt(ms + NORM_EPS) * gmix_ref[...]).astype(BF16)

    z = jnp.dot(h_scr[...], w_ref[...], preferred_element_type=F32)
    t_qa, t_va, t_qb, t_vb = COL_QA // TN, COL_VA // TN, COL_QB // TN, COL_VB // TN
    is_gate = j < t_qa
    is_norm = ((j >= t_qa) & (j < t_va)) | ((j >= t_qb) & (j < t_vb))

    @pl.when(is_gate)
    def _():
        z_ref[...] = jax.nn.sigmoid(z).astype(BF16)

    @pl.when(is_norm)
    def _():
        ms = jnp.dot((z * z).astype(BF16), gmat_ref[...], preferred_element_type=F32)
        z_ref[...] = (z * lax.rsqrt(ms + NORM_EPS) * gvec_ref[...]).astype(BF16)

    @pl.when(jnp.logical_not(is_gate | is_norm))
    def _():
        z_ref[...] = z.astype(BF16)


def _in_proj(x2, gmix, w_in_p, gvec, gmat, tm):
    T = x2.shape[0]
    t_qb = COL_QB // TN
    return pl.pallas_call(
        _in_proj_kernel,
        grid=(T // tm, N_TILES),
        in_specs=[
            pl.BlockSpec((tm, D_MODEL), lambda i, j: (i, 0)),
            pl.BlockSpec((1, D_MODEL), lambda i, j: (0, 0)),
            pl.BlockSpec((D_MODEL, TN), lambda i, j: (0, j)),
            pl.BlockSpec((1, TN), lambda i, j: (0, j)),
            pl.BlockSpec((None, TN, TN), lambda i, j: (jnp.where(j < t_qb, 0, 1), 0, 0)),
        ],
        out_specs=pl.BlockSpec((tm, TN), lambda i, j: (i, j)),
        out_shape=jax.ShapeDtypeStruct((T, N_IN), BF16),
        scratch_shapes=[pltpu.VMEM((tm, D_MODEL), BF16)],
        compiler_params=pltpu.CompilerParams(
            dimension_semantics=("parallel", "arbitrary"), vmem_limit_bytes=VMEM_LIMIT),
        name="in_proj",
    )(x2, gmix, w_in_p, gvec, gmat)


LOG2E = math.log2(math.e)
AUG_P, AUG_J, AUG_R = 0, 3, 6
REF_HEADROOM = 60.0
BOUND_LIMIT = 80.0
MASKED = -30000.0


def _split3(x):
    hi = x.astype(BF16).astype(F32)
    r = x - hi
    mid = r.astype(BF16).astype(F32)
    lo = (r - mid).astype(BF16).astype(F32)
    return hi, mid, lo


def _key_table(tk):
    hi, mid, lo = _split3(jnp.arange(tk, dtype=F32) * LOG2E)
    one = jnp.ones((tk,), F32)
    cols = {AUG_P: one, AUG_P + 1: one, AUG_P + 2: one, AUG_J: hi, AUG_J + 1: mid, AUG_J + 2: lo, AUG_R: one}
    zero = jnp.zeros((tk,), F32)
    return jnp.stack([cols.get(l, zero) for l in range(LANES)], axis=1).astype(BF16)


def _diff_attn_kernel(slopes_ref, lamv_ref, q_ref, k_ref, v_ref, ktab_ref, gsub_ref, o_ref,
                      kcat, vcat, kstat, acc, m_scr, *, tq, tk, seq, lam_init):
    h = pl.program_id(1)
    i = pl.program_id(2)
    n_chunks = seq // tk
    slope = slopes_ref[h]
    cs = slope * LOG2E
    nt = (((1,), (1,)), ((), ()))

    @pl.when(i == 0)
    def _():
        lane_k = lax.broadcasted_iota(jnp.int32, (tk, LANES), 1)
        ones_col = jnp.where(lane_k == 0, 1.0, 0.0).astype(BF16)

        def fill(c, carry):
            n1, n2 = carry
            sl = pl.ds(pl.multiple_of(c * tk, tk), tk)
            kc = k_ref[sl, :]
            kcat[sl, 0:LANES] = kc
            kcat[sl, LANES:2 * LANES] = ktab_ref[...]
            vcat[sl, 0:LANES] = v_ref[sl, :]
            vcat[sl, LANES:2 * LANES] = ones_col
            kf = kc.astype(F32)
            kk = kf * kf
            s1 = jnp.sum(jnp.where(lane_k < HD_A, kk, 0.0), axis=-1, keepdims=True)
            s2 = jnp.sum(jnp.where(lane_k >= HD_A, kk, 0.0), axis=-1, keepdims=True)
            return jnp.maximum(n1, s1), jnp.maximum(n2, s2)

        z = jnp.zeros((tk, 1), F32)
        n1, n2 = lax.fori_loop(0, n_chunks, fill, (z, z))
        kstat[0:1, :] = jnp.broadcast_to(jnp.sqrt(jnp.max(n1, axis=0, keepdims=True)), (1, LANES))
        kstat[1:2, :] = jnp.broadcast_to(jnp.sqrt(jnp.max(n2, axis=0, keepdims=True)), (1, LANES))

    q = q_ref[...]
    lane = lax.broadcasted_iota(jnp.int32, (tq, LANES), 1)
    qf = q.astype(F32)
    qq = qf * qf
    qn1 = jnp.sqrt(jnp.sum(jnp.where(lane < HD_A, qq, 0.0), axis=-1, keepdims=True))
    qn2 = jnp.sqrt(jnp.sum(jnp.where(lane >= HD_A, qq, 0.0), axis=-1, keepdims=True))
    bound = jnp.concatenate([qn1 * kstat[0:1, 0:1], qn2 * kstat[1:2, 0:1]], axis=0) * 1.01
    bound_ok = jnp.max(bound) <= BOUND_LIMIT

    zero = jnp.zeros_like(q)
    base = jnp.concatenate([jnp.where(lane < HD_A, q, zero), jnp.where(lane >= HD_A, q, zero)], axis=0)
    row2 = lax.broadcasted_iota(jnp.int32, (2 * tq, LANES), 0)
    lane2 = lax.broadcasted_iota(jnp.int32, (2 * tq, LANES), 1)
    iq = jnp.where(row2 >= tq, row2 - tq, row2).astype(F32)
    hi, mid, lo = _split3(cs * iq)
    ppos = jnp.where(lane2 == AUG_P, hi, jnp.where(lane2 == AUG_P + 1, mid,
                                                   jnp.where(lane2 == AUG_P + 2, lo, 0.0)))
    pslope = jnp.where((lane2 >= AUG_J) & (lane2 < AUG_J + 3), slope, 0.0)
    rcol = jnp.where(lane2 == AUG_R, REF_HEADROOM - bound, 0.0)
    q_left = jnp.concatenate([base, (rcol - ppos + pslope).astype(BF16)], axis=1)
    q_right = jnp.concatenate([base, (rcol + ppos - pslope).astype(BF16)], axis=1)
    q_diag = jnp.concatenate([base, rcol.astype(BF16)], axis=1)
    q_plain = jnp.concatenate([base, jnp.zeros_like(base)], axis=1)

    row = lax.broadcasted_iota(jnp.int32, (2 * tq, tk), 0)
    col = lax.broadcasted_iota(jnp.int32, (2 * tq, tk), 1)
    rel = (jnp.where(row >= tq, row - tq, row) - col).astype(F32)
    i0 = i * tq
    cd = i0 // tk

    def chunk(c):
        return pl.ds(pl.multiple_of(c * tk, tk), tk)

    def fixed_reference():
        acc[...] = jnp.zeros(acc.shape, F32)

        def body(c, carry):
            left = c < cd
            off = cs * (i0 - c * tk).astype(F32)
            d = jnp.where(c == cd, MASKED, jnp.where(left, -off, off))
            s = lax.dot_general(jnp.where(left, q_left, q_right), kcat[chunk(c), :], nt,
                                preferred_element_type=F32) + d
            acc[...] += jnp.dot(jnp.exp2(s).astype(BF16), vcat[chunk(c), :], preferred_element_type=F32)
            return carry

        lax.fori_loop(0, n_chunks, body, 0, unroll=2)
        s = lax.dot_general(q_diag, kcat[chunk(cd), :], nt, preferred_element_type=F32)
        s = s - cs * jnp.abs(rel + (i0 - cd * tk).astype(F32))
        acc[...] += jnp.dot(jnp.exp2(s).astype(BF16), vcat[chunk(cd), :], preferred_element_type=F32)

    def online_reference():
        acc[...] = jnp.zeros(acc.shape, F32)
        m_scr[...] = jnp.full(m_scr.shape, NEG_INF, F32)

        def body(c, carry):
            s = lax.dot_general(q_plain, kcat[chunk(c), :], nt, preferred_element_type=F32)
            s = s - cs * jnp.abs(rel + (i0 - c * tk).astype(F32))
            m_prev = m_scr[...]
            m_new = jnp.maximum(m_prev, jnp.max(s, axis=-1, keepdims=True))
            p = jnp.exp2(s - m_new)
            acc[...] = jnp.exp2(m_prev - m_new) * acc[...] + jnp.dot(
                p.astype(BF16), vcat[chunk(c), :], preferred_element_type=F32)
            m_scr[...] = m_new
            return carry

        lax.fori_loop(0, n_chunks, body, 0)

    lax.cond(bound_ok, fixed_reference, online_reference)

    lamv = lamv_ref[...]
    lam = (jnp.exp(jnp.sum(lamv[0:1] * lamv[1:2], axis=-1, keepdims=True))
           - jnp.exp(jnp.sum(lamv[2:3] * lamv[3:4], axis=-1, keepdims=True)) + lam_init)
    a = acc[...]
    o = a[:, 0:LANES] / a[:, LANES:LANES + 1]
    d = o[:tq] - lam * o[tq:]
    ms = jnp.mean(d * d, axis=-1, keepdims=True)
    o_ref[...] = (d * lax.rsqrt(ms + SUBLN_EPS) * gsub_ref[...] * (1.0 - lam_init)).astype(BF16)


def _diff_attn(z3, slopes_a, lamv, gsub, lam_init, tq, tk):
    B, S, _ = z3.shape
    assert tk % tq == 0 and S % tk == 0
    qb0, kb0, vb0 = COL_QA // LANES, COL_KA // LANES, COL_VA // LANES
    kernel = functools.partial(_diff_attn_kernel, tq=tq, tk=tk, seq=S, lam_init=lam_init)
    return pl.pallas_call(
        kernel,
        grid=(B, H_A, S // tq),
        in_specs=[
            pl.BlockSpec(memory_space=pltpu.SMEM),
            pl.BlockSpec((4, HD_A), lambda b, h, i: (0, 0)),
            pl.BlockSpec((None, tq, LANES), lambda b, h, i: (b, i, qb0 + h)),
            pl.BlockSpec((None, S, LANES), lambda b, h, i: (b, 0, kb0 + h)),
            pl.BlockSpec((None, S, LANES), lambda b, h, i: (b, 0, vb0 + h)),
            pl.BlockSpec((tk, LANES), lambda b, h, i: (0, 0)),
            pl.BlockSpec((1, LANES), lambda b, h, i: (0, 0)),
        ],
        out_specs=pl.BlockSpec((None, tq, LANES), lambda b, h, i: (b, i, h)),
        out_shape=jax.ShapeDtypeStruct((B, S, OA_W), BF16),
        scratch_shapes=[pltpu.VMEM((S, 2 * LANES), BF16), pltpu.VMEM((S, 2 * LANES), BF16),
                        pltpu.VMEM((8, LANES), F32), pltpu.VMEM((2 * tq, 2 * LANES), F32),
                        pltpu.VMEM((2 * tq, 1), F32)],
        compiler_params=pltpu.CompilerParams(
            dimension_semantics=("arbitrary", "arbitrary", "arbitrary"), vmem_limit_bytes=VMEM_LIMIT),
        name="diff_attn",
    )(slopes_a, lamv, z3, z3, z3, _key_table(tk), gsub)


def _band_attn_kernel(q_ref, kp_ref, kc_ref, kn_ref, vp_ref, vc_ref, vn_ref, o_ref, lse_ref,
                      *, length, dil, half, slopes):
    i = pl.program_id(2)
    qb = Q_BLOCK
    kw = 3 * qb
    row = lax.broadcasted_iota(jnp.int32, (qb, kw), 0)
    col = lax.broadcasted_iota(jnp.int32, (qb, kw), 1)
    rel = col - qb - row
    kpos = (i - 1) * qb + col
    valid = (jnp.abs(rel) <= half) & (kpos >= 0) & (kpos < length)
    dist = (dil * jnp.abs(rel)).astype(F32)
    for hh in range(H_B):
        sl = slice(hh * HD_B, (hh + 1) * HD_B)
        kk = jnp.concatenate([kp_ref[:, sl], kc_ref[:, sl], kn_ref[:, sl]], axis=0)
        vv = jnp.concatenate([vp_ref[:, sl], vc_ref[:, sl], vn_ref[:, sl]], axis=0)
        s = lax.dot_general(q_ref[:, sl], kk, (((1,), (1,)), ((), ())), preferred_element_type=F32)
        s = jnp.where(valid, s - float(slopes[hh]) * dist, NEG_INF)
        m = jnp.max(s, axis=-1, keepdims=True)
        p = jnp.exp(s - m)
        l = jnp.sum(p, axis=-1, keepdims=True)
        o = jnp.dot(p.astype(BF16), vv, preferred_element_type=F32) / l
        o_ref[:, sl] = o.astype(o_ref.dtype)
        lse_ref[:, sl] = jnp.broadcast_to(m + jnp.log(l), (qb, HD_B))


def _band_attn(z3, g, window, dil):
    B, S, _ = z3.shape
    L = S // dil
    half = (window // 2) // dil
    nb = L // Q_BLOCK
    zs = z3.reshape(B, L, dil * N_IN)
    per_tok = N_IN // TN
    qo, ko, vo = COL_QB // TN + g, COL_KB // TN + g, COL_VB // TN + g
    slopes = _alibi_slopes(N_GROUPS_B * H_B).reshape(N_GROUPS_B, H_B)[g]
    kernel = functools.partial(_band_attn_kernel, length=L, dil=dil, half=half, slopes=slopes)

    def spec(off, shift):
        def imap(b, c, i):
            return (b, jnp.clip(i + shift, 0, nb - 1), c * per_tok + off)
        return pl.BlockSpec((None, Q_BLOCK, TN), imap)

    out_spec = pl.BlockSpec((None, Q_BLOCK, OB_W), lambda b, c, i: (b, i, c))
    o, lse = pl.pallas_call(
        kernel,
        grid=(B, dil, nb),
        in_specs=[spec(qo, 0), spec(ko, -1), spec(ko, 0), spec(ko, 1),
                  spec(vo, -1), spec(vo, 0), spec(vo, 1)],
        out_specs=[out_spec, out_spec],
        out_shape=[jax.ShapeDtypeStruct((B, L, dil * OB_W), BF16),
                   jax.ShapeDtypeStruct((B, L, dil * OB_W), F32)],
        compiler_params=pltpu.CompilerParams(
            dimension_semantics=("parallel", "parallel", "arbitrary"), vmem_limit_bytes=VMEM_LIMIT),
        name=f"band_attn_g{g}",
    )(zs, zs, zs, zs, zs, zs, zs)
    return o.reshape(B * S, OB_W), lse.reshape(B * S, OB_W)


def _merge_kernel(x_ref, ga_ref, gb_ref, oa_ref, o0_ref, o1_ref, o2_ref, l0_ref, l1_ref, l2_ref,
                  wpa_ref, wpb_ref, wo_ref, gffn_ref, x1_ref, h2_ref):
    l0, l1, l2 = l0_ref[...], l1_ref[...], l2_ref[...]
    m = jnp.maximum(jnp.maximum(l0, l1), l2)
    w0, w1, w2 = jnp.exp(l0 - m), jnp.exp(l1 - m), jnp.exp(l2 - m)
    ob = (w0 * o0_ref[...].astype(F32) + w1 * o1_ref[...].astype(F32)
          + w2 * o2_ref[...].astype(F32)) / (w0 + w1 + w2)
    pa = jnp.dot(oa_ref[...], wpa_ref[...], preferred_element_type=F32)
    pb = jnp.dot(ob.astype(BF16), wpb_ref[...], preferred_element_type=F32)
    merged = ga_ref[...].astype(F32) * pa + gb_ref[...].astype(F32) * pb
    x1 = x_ref[...] + jnp.dot(merged.astype(BF16), wo_ref[...], preferred_element_type=F32)
    x1_ref[...] = x1
    ms = jnp.mean(x1 * x1, axis=-1, keepdims=True)
    h2_ref[...] = (x1 * lax.rsqrt(ms + NORM_EPS) * gffn_ref[...]).astype(BF16)


def _merge(x2, z2, oa2, obs, lses, wpa, wpb, wo, gffn, tm):
    T = x2.shape[0]
    row = lambda w: pl.BlockSpec((tm, w), lambda i: (i, 0))
    const = lambda a: pl.BlockSpec(a.shape, lambda i: (0, 0), pipeline_mode=pl.Buffered(1))
    return pl.pallas_call(
        _merge_kernel,
        grid=(T // tm,),
        in_specs=[row(D_MODEL),
                  pl.BlockSpec((tm, D_MODEL), lambda i: (i, COL_GATE_A // D_MODEL)),
                  pl.BlockSpec((tm, D_MODEL), lambda i: (i, COL_GATE_B // D_MODEL)),
                  row(OA_W), row(OB_W), row(OB_W), row(OB_W), row(OB_W), row(OB_W), row(OB_W),
                  const(wpa), const(wpb), const(wo), const(gffn)],
        out_specs=[row(D_MODEL), row(D_MODEL)],
        out_shape=[jax.ShapeDtypeStruct((T, D_MODEL), F32), jax.ShapeDtypeStruct((T, D_MODEL), BF16)],
        compiler_params=pltpu.CompilerParams(
            dimension_semantics=("parallel",), vmem_limit_bytes=VMEM_LIMIT),
        name="merge",
    )(x2, z2, z2, oa2, *obs, *lses, wpa, wpb, wo, gffn)


HALO = 16


def _ffn_kernel(h_ref, hp_ref, hn_ref, x1_ref, wg_ref, wv_ref, cwg_ref, cwv_ref, cbg_ref, cbv_ref,
                wd_ref, o_ref, hcat_scr, acc_scr, *, tm, tiles_per_seq, n_f):
    i = pl.program_id(0)
    f = pl.program_id(1)

    @pl.when(f == 0)
    def _():
        pos = i % tiles_per_seq
        hp = hp_ref[...]
        hn = hn_ref[...]
        hcat_scr[0:HALO, :] = jnp.where(pos == 0, jnp.zeros_like(hp), hp)
        hcat_scr[HALO:HALO + tm, :] = h_ref[...]
        hcat_scr[HALO + tm:, :] = jnp.where(pos == tiles_per_seq - 1, jnp.zeros_like(hn), hn)
        acc_scr[...] = jnp.zeros(acc_scr.shape, F32)

    hcat = hcat_scr[...]
    rows = tm + 2 * HALO

    def conv_half(w_ref, cw_ref, cb_ref):
        a = jnp.dot(hcat, w_ref[...], preferred_element_type=F32)
        a_prev = pltpu.roll(a, 1, 0)[HALO:HALO + tm]
        a_next = pltpu.roll(a, rows - 1, 0)[HALO:HALO + tm]
        cw = cw_ref[...]
        return a_prev * cw[0:1] + a[HALO:HALO + tm] * cw[1:2] + a_next * cw[2:3] + cb_ref[...]

    ug = conv_half(wg_ref, cwg_ref, cbg_ref)
    uv = conv_half(wv_ref, cwv_ref, cbv_ref)
    act = 0.5 * ug * (1.0 + lax.erf(ug * np.float32(math.sqrt(0.5)))) * uv
    acc_scr[...] += jnp.dot(act.astype(BF16), wd_ref[...], preferred_element_type=F32)

    @pl.when(f == n_f - 1)
    def _():
        o_ref[...] = x1_ref[...] + acc_scr[...]


def _ffn(h2, x1, w_up, conv_w, conv_b, w_down, seq, tm, tf):
    T = h2.shape[0]
    n_f = D_FF // tf
    hb = tm // HALO
    n_hb = T // HALO
    kernel = functools.partial(_ffn_kernel, tm=tm, tiles_per_seq=seq // tm, n_f=n_f)
    return pl.pallas_call(
        kernel,
        grid=(T // tm, n_f),
        in_specs=[
            pl.BlockSpec((tm, D_MODEL), lambda i, f: (i, 0)),
            pl.BlockSpec((HALO, D_MODEL), lambda i, f: (jnp.maximum(i * hb - 1, 0), 0)),
            pl.BlockSpec((HALO, D_MODEL), lambda i, f: (jnp.minimum((i + 1) * hb, n_hb - 1), 0)),
            pl.BlockSpec((tm, D_MODEL), lambda i, f: (i, 0)),
            pl.BlockSpec((D_MODEL, tf), lambda i, f: (0, f)),
            pl.BlockSpec((D_MODEL, tf), lambda i, f: (0, f + n_f)),
            pl.BlockSpec((3, tf), lambda i, f: (0, f)),
            pl.BlockSpec((3, tf), lambda i, f: (0, f + n_f)),
            pl.BlockSpec((1, tf), lambda i, f: (0, f)),
            pl.BlockSpec((1, tf), lambda i, f: (0, f + n_f)),
            pl.BlockSpec((tf, D_MODEL), lambda i, f: (f, 0)),
        ],
        out_specs=pl.BlockSpec((tm, D_MODEL), lambda i, f: (i, 0)),
        out_shape=jax.ShapeDtypeStruct((T, D_MODEL), F32),
        scratch_shapes=[pltpu.VMEM((tm + 2 * HALO, D_MODEL), BF16), pltpu.VMEM((tm, D_MODEL), F32)],
        compiler_params=pltpu.CompilerParams(
            dimension_semantics=("parallel", "arbitrary"), vmem_limit_bytes=VMEM_LIMIT),
        name="ffn",
    )(h2, h2, h2, x1, w_up, w_up, conv_w, conv_w, conv_b, conv_b, w_down)


def _prep_params(g_mix_norm, w_in, g_qa, g_ka, lam_q1, lam_k1, lam_q2, lam_k2, g_subln, g_qb, g_kb,
                 w_pa, w_pb, w_o, g_ffn_norm, w_up, conv_w, conv_b, w_down):
    qa, ka, va, qb, kb, vb, gate_a, gate_b = jnp.split(
        w_in, list(np.cumsum([QA_W, QA_W, QA_W, QB_W, QB_W, QB_W, D_MODEL])), axis=-1)
    w_in_p = jnp.concatenate([gate_a, gate_b, qa, ka, va, qb, kb, vb], axis=-1).astype(BF16)
    ones = lambda n: jnp.ones((n,), F32)
    gvec = jnp.concatenate([
        ones(2 * D_MODEL),
        jnp.tile(g_qa.astype(F32), 2 * H_A) * (HD_A ** -0.5 * LOG2E),
        jnp.tile(g_ka.astype(F32), 2 * H_A),
        ones(QA_W),
        jnp.tile(g_qb.astype(F32), N_GROUPS_B * H_B) * (HD_B ** -0.5),
        jnp.tile(g_kb.astype(F32), N_GROUPS_B * H_B),
        ones(QB_W)]).reshape(1, N_IN)
    idx = np.arange(TN)
    gmat = jnp.asarray(np.stack([
        (idx[:, None] // HD_A == idx[None, :] // HD_A) / HD_A,
        (idx[:, None] // HD_B == idx[None, :] // HD_B) / HD_B]).astype(np.float32), BF16)
    lamv = jnp.stack([lam_q1, lam_k1, lam_q2, lam_k2]).astype(F32)
    return dict(
        gmix=g_mix_norm.astype(F32).reshape(1, D_MODEL), w_in=w_in_p, gvec=gvec, gmat=gmat, lamv=lamv,
        gsub=g_subln.astype(F32).reshape(1, 2 * HD_A), wpa=w_pa.astype(BF16), wpb=w_pb.astype(BF16),
        wo=w_o.astype(BF16), gffn=g_ffn_norm.astype(F32).reshape(1, D_MODEL), w_up=w_up.astype(BF16),
        conv_w=conv_w.astype(F32), conv_b=conv_b.astype(F32).reshape(1, 2 * D_FF),
        w_down=w_down.astype(BF16))


def _encoder_layer(x, layer_idx, p):
    B, S, _ = x.shape
    T = B * S
    lam_init = 0.8 - 0.6 * math.exp(-0.3 * layer_idx)
    x2 = x.reshape(T, D_MODEL)
    z2 = _in_proj(x2, p["gmix"], p["w_in"], p["gvec"], p["gmat"], tm=min(1024, T))
    z3 = z2.reshape(B, S, N_IN)
    oa = _diff_attn(z3, jnp.asarray(_alibi_slopes(H_A)), p["lamv"], p["gsub"], lam_init,
                    tq=256, tk=512)
    obs, lses = [], []
    for g, (window, dil) in enumerate(DIL_CONFIGS):
        o, lse = _band_attn(z3, g, window, dil)
        obs.append(o)
        lses.append(lse)
    x1, h2 = _merge(x2, z2, oa.reshape(T, OA_W), obs, lses, p["wpa"], p["wpb"], p["wo"], p["gffn"],
                    tm=256)
    y = _ffn(h2, x1, p["w_up"], p["conv_w"], p["conv_b"], p["w_down"], seq=S, tm=512, tf=512)
    return y.reshape(B, S, D_MODEL)


def kernel(x_prompt, x_sample, g_mix_norm, w_in, g_qa, g_ka, lam_q1, lam_k1, lam_q2, lam_k2, g_subln,
           g_qb, g_kb, w_pa, w_pb, w_o, g_ffn_norm, w_up, conv_w, conv_b, w_down):
    y_prompt, y_sample = x_prompt, x_sample
    for l in range(w_in.shape[0]):
        p = _prep_params(g_mix_norm[l], w_in[l], g_qa[l], g_ka[l], lam_q1[l], lam_k1[l], lam_q2[l],
                         lam_k2[l], g_subln[l], g_qb[l], g_kb[l], w_pa[l], w_pb[l], w_o[l],
                         g_ffn_norm[l], w_up[l], conv_w[l], conv_b[l], w_down[l])
        y_prompt = _encoder_layer(y_prompt, l, p)
        y_sample = _encoder_layer(y_sample, l, p)
    return (y_prompt, y_sample)
```

```python
import functools
import math

import numpy as np
import jax
import jax.numpy as jnp
from jax import lax
from jax.experimental import pallas as pl
from jax.experimental.pallas import tpu as pltpu

D_MODEL = 2048
H_A = 8
HD_A = 64
DIL_CONFIGS = ((128, 1), (512, 4), (2048, 16))
N_GROUPS_B = 3
H_B = 4
HD_B = 128
D_FF = 5632
Q_BLOCK = 128
NORM_EPS = 1e-6
SUBLN_EPS = 1e-5
NEG_INF = -1e30

QA_W = H_A * 2 * HD_A
QB_W = N_GROUPS_B * H_B * HD_B
OA_W = QA_W
OB_W = H_B * HD_B
N_IN = 3 * QA_W + 3 * QB_W + 2 * D_MODEL

LANES = 128
COL_GATE_A = 0
COL_GATE_B = D_MODEL
COL_QA = 2 * D_MODEL
COL_KA = COL_QA + QA_W
COL_VA = COL_KA + QA_W
COL_QB = COL_VA + QA_W
COL_KB = COL_QB + QB_W
COL_VB = COL_KB + QB_W
TN = 512
N_TILES = N_IN // TN
VMEM_LIMIT = 56 * 1024 * 1024

F32 = jnp.float32
BF16 = jnp.bfloat16


def _alibi_slopes(n):
    return np.asarray(2.0 ** (-8.0 * (np.arange(n) + 1) / n), dtype=np.float32)


T_QA, T_VA, T_QB, T_KB, T_VB = COL_QA // TN, COL_VA // TN, COL_QB // TN, COL_KB // TN, COL_VB // TN
DILS = tuple(d for _, d in DIL_CONFIGS)


def _in_proj_kernel(x_ref, gmix_ref, w_ref, gvec_ref, gmat_ref, z_ref, g0_ref, g1_ref, g2_ref,
                    h_scr, t_scr, *, tm):
    j = pl.program_id(1)

    @pl.when(j == 0)
    def _():
        x = x_ref[...]
        ms = jnp.mean(x * x, axis=-1, keepdims=True)
        h_scr[...] = (x * lax.rsqrt(ms + NORM_EPS) * gmix_ref[...]).astype(BF16)

    z = jnp.dot(h_scr[...], w_ref[...], preferred_element_type=F32)

    def normed():
        ms = jnp.dot((z * z).astype(BF16), gmat_ref[...], preferred_element_type=F32)
        return z * lax.rsqrt(ms + NORM_EPS) * gvec_ref[...]

    @pl.when(j < T_QA)
    def _():
        z_ref[...] = jax.nn.sigmoid(z).astype(BF16)

    @pl.when((j >= T_QA) & (j < T_VA))
    def _():
        z_ref[...] = normed().astype(BF16)

    @pl.when((j >= T_VA) & (j < T_QB))
    def _():
        z_ref[...] = z.astype(BF16)

    def scatter(g_ref, dil, val):
        if dil == 1:
            g_ref[0] = val.astype(BF16)
        else:
            for kb in range(TN // LANES):
                t_scr[kb] = val[:, kb * LANES:(kb + 1) * LANES]
            for c in range(dil):
                for kb in range(TN // LANES):
                    g_ref[c, :, kb * LANES:(kb + 1) * LANES] = (
                        t_scr[kb, pl.ds(c, tm // dil, stride=dil), :].astype(BF16))

    for g, (g_ref, dil) in enumerate(zip((g0_ref, g1_ref, g2_ref), DILS)):
        @pl.when((j == T_QB + g) | (j == T_KB + g))
        def _(g_ref=g_ref, dil=dil):
            scatter(g_ref, dil, normed())

        @pl.when(j == T_VB + g)
        def _(g_ref=g_ref, dil=dil):
            scatter(g_ref, dil, z)


def _in_proj(x2, gmix, w_in_p, gvec, gmat, batch, seq, tm):
    T = x2.shape[0]
    tiles_per_seq = seq // tm
    main_w = COL_QB

    def group_spec(g, dil):
        def imap(i, j):
            sec = (j >= T_KB + g).astype(jnp.int32) + (j >= T_VB + g).astype(jnp.int32)
            return (sec, i // tiles_per_seq, 0, i % tiles_per_seq, 0)
        return pl.BlockSpec((None, None, dil, tm // dil, TN), imap)

    return pl.pallas_call(
        functools.partial(_in_proj_kernel, tm=tm),
        grid=(T // tm, N_TILES),
        in_specs=[
            pl.BlockSpec((tm, D_MODEL), lambda i, j: (i, 0)),
            pl.BlockSpec((1, D_MODEL), lambda i, j: (0, 0)),
            pl.BlockSpec((D_MODEL, TN), lambda i, j: (0, j)),
            pl.BlockSpec((1, TN), lambda i, j: (0, j)),
            pl.BlockSpec((None, TN, TN), lambda i, j: (jnp.where(j < T_QB, 0, 1), 0, 0)),
        ],
        out_specs=[pl.BlockSpec((tm, TN), lambda i, j: (i, jnp.minimum(j, T_QB - 1)))]
                  + [group_spec(g, dil) for g, dil in enumerate(DILS)],
        out_shape=[jax.ShapeDtypeStruct((T, main_w), BF16)]
                  + [jax.ShapeDtypeStruct((3, batch, dil, seq // dil, TN), BF16) for dil in DILS],
        scratch_shapes=[pltpu.VMEM((tm, D_MODEL), BF16), pltpu.VMEM((TN // LANES, tm, LANES), F32)],
        compiler_params=pltpu.CompilerParams(
            dimension_semantics=("arbitrary", "arbitrary"), vmem_limit_bytes=VMEM_LIMIT),
        name="in_proj",
    )(x2, gmix, w_in_p, gvec, gmat)


LOG2E = math.log2(math.e)
AUG_P, AUG_J, AUG_R = 0, 3, 6
REF_HEADROOM = 60.0
BOUND_LIMIT = 80.0
MASKED = -30000.0


def _split3(x):
    def top(a):
        bits = lax.bitcast_convert_type(a, jnp.uint32) & jnp.uint32(0xFFFF0000)
        return lax.bitcast_convert_type(bits, F32)
    hi = top(x)
    mid = top(x - hi)
    lo = top(x - hi - mid)
    return hi, mid, lo


def _key_table(tk):
    def top(a):
        return (a.view(np.uint32) & np.uint32(0xFFFF0000)).view(np.float32)
    x = (np.arange(tk, dtype=np.float32) * np.float32(LOG2E)).astype(np.float32)
    hi = top(x)
    mid = top(x - hi)
    lo = top(x - hi - mid)
    tab = np.zeros((tk, LANES), np.float32)
    tab[:, AUG_P:AUG_P + 3] = 1.0
    tab[:, AUG_J], tab[:, AUG_J + 1], tab[:, AUG_J + 2] = hi, mid, lo
    tab[:, AUG_R] = 1.0
    return jnp.asarray(tab).astype(BF16)


def _diff_attn_kernel(slopes_ref, lamv_ref, q_ref, k_ref, v_ref, ktab_ref, gsub_ref, o_ref,
                      kcat, vcat, kstat, acc, m_scr, *, tq, tk, seq, lam_init):
    h = pl.program_id(1)
    i = pl.program_id(2)
    n_chunks = seq // tk
    slope = slopes_ref[h]
    cs = slope * LOG2E
    nt = (((1,), (1,)), ((), ()))

    @pl.when(i == 0)
    def _():
        lane_k = lax.broadcasted_iota(jnp.int32, (tk, LANES), 1)
        ones_col = jnp.where(lane_k == 0, 1.0, 0.0).astype(BF16)

        def fill(c, carry):
            n1, n2 = carry
            sl = pl.ds(pl.multiple_of(c * tk, tk), tk)
            kc = k_ref[sl, :]
            kcat[sl, 0:LANES] = kc
            kcat[sl, LANES:2 * LANES] = ktab_ref[...]
            vcat[sl, 0:LANES] = v_ref[sl, :]
            vcat[sl, LANES:2 * LANES] = ones_col
            kf = kc.astype(F32)
            kk = kf * kf
            s1 = jnp.sum(jnp.where(lane_k < HD_A, kk, 0.0), axis=-1, keepdims=True)
            s2 = jnp.sum(jnp.where(lane_k >= HD_A, kk, 0.0), axis=-1, keepdims=True)
            return jnp.maximum(n1, s1), jnp.maximum(n2, s2)

        z = jnp.zeros((tk, 1), F32)
        n1, n2 = lax.fori_loop(0, n_chunks, fill, (z, z))
        kstat[0:1, :] = jnp.broadcast_to(jnp.sqrt(jnp.max(n1, axis=0, keepdims=True)), (1, LANES))
        kstat[1:2, :] = jnp.broadcast_to(jnp.sqrt(jnp.max(n2, axis=0, keepdims=True)), (1, LANES))

    q = q_ref[...]
    lane = lax.broadcasted_iota(jnp.int32, (tq, LANES), 1)
    qf = q.astype(F32)
    qq = qf * qf
    qn1 = jnp.sqrt(jnp.sum(jnp.where(lane < HD_A, qq, 0.0), axis=-1, keepdims=True))
    qn2 = jnp.sqrt(jnp.sum(jnp.where(lane >= HD_A, qq, 0.0), axis=-1, keepdims=True))
    bound = jnp.concatenate([qn1 * kstat[0:1, 0:1], qn2 * kstat[1:2, 0:1]], axis=0) * 1.01
    bound_ok = jnp.max(bound) <= BOUND_LIMIT

    zero = jnp.zeros_like(q)
    base = jnp.concatenate([jnp.where(lane < HD_A, q, zero), jnp.where(lane >= HD_A, q, zero)], axis=0)
    row2 = lax.broadcasted_iota(jnp.int32, (2 * tq, LANES), 0)
    lane2 = lax.broadcasted_iota(jnp.int32, (2 * tq, LANES), 1)
    iq = jnp.where(row2 >= tq, row2 - tq, row2).astype(F32)
    hi, mid, lo = _split3(cs * iq)
    ppos = jnp.where(lane2 == AUG_P, hi, jnp.where(lane2 == AUG_P + 1, mid,
                                                   jnp.where(lane2 == AUG_P + 2, lo, 0.0)))
    pslope = jnp.where((lane2 >= AUG_J) & (lane2 < AUG_J + 3), slope, 0.0)
    rcol = jnp.where(lane2 == AUG_R, REF_HEADROOM - bound, 0.0)
    q_left = jnp.concatenate([base, (rcol - ppos + pslope).astype(BF16)], axis=1)
    q_right = jnp.concatenate([base, (rcol + ppos - pslope).astype(BF16)], axis=1)
    q_diag = jnp.concatenate([base, rcol.astype(BF16)], axis=1)
    q_plain = jnp.concatenate([base, jnp.zeros_like(base)], axis=1)

    row = lax.broadcasted_iota(jnp.int32, (2 * tq, tk), 0)
    col = lax.broadcasted_iota(jnp.int32, (2 * tq, tk), 1)
    rel = (jnp.where(row >= tq, row - tq, row) - col).astype(F32)
    i0 = i * tq
    cd = i0 // tk

    def chunk(c):
        return pl.ds(pl.multiple_of(c * tk, tk), tk)

    def fixed_reference():
        acc[...] = jnp.zeros(acc.shape, F32)

        def body(c, carry):
            left = c < cd
            off = cs * (i0 - c * tk).astype(F32)
            d = jnp.where(c == cd, MASKED, jnp.where(left, -off, off))
            s = lax.dot_general(jnp.where(left, q_left, q_right), kcat[chunk(c), :], nt,
                                preferred_element_type=F32) + d
            acc[...] += jnp.dot(jnp.exp2(s).astype(BF16), vcat[chunk(c), :], preferred_element_type=F32)
            return carry

        lax.fori_loop(0, n_chunks, body, 0, unroll=2)
        s = lax.dot_general(q_diag, kcat[chunk(cd), :], nt, preferred_element_type=F32)
        s = s - cs * jnp.abs(rel + (i0 - cd * tk).astype(F32))
        acc[...] += jnp.dot(jnp.exp2(s).astype(BF16), vcat[chunk(cd), :], preferred_element_type=F32)

    def online_reference():
        acc[...] = jnp.zeros(acc.shape, F32)
        m_scr[...] = jnp.full(m_scr.shape, NEG_INF, F32)

        def body(c, carry):
            s = lax.dot_general(q_plain, kcat[chunk(c), :], nt, preferred_element_type=F32)
            s = s - cs * jnp.abs(rel + (i0 - c * tk).astype(F32))
            m_prev = m_scr[...]
            m_new = jnp.maximum(m_prev, jnp.max(s, axis=-1, keepdims=True))
            p = jnp.exp2(s - m_new)
            acc[...] = jnp.exp2(m_prev - m_new) * acc[...] + jnp.dot(
                p.astype(BF16), vcat[chunk(c), :], preferred_element_type=F32)
            m_scr[...] = m_new
            return carry

        lax.fori_loop(0, n_chunks, body, 0)

    lax.cond(bound_ok, fixed_reference, online_reference)

    lamv = lamv_ref[...]
    lam = (jnp.exp(jnp.sum(lamv[0:1] * lamv[1:2], axis=-1, keepdims=True))
           - jnp.exp(jnp.sum(lamv[2:3] * lamv[3:4], axis=-1, keepdims=True)) + lam_init)
    a = acc[...]
    o = a[:, 0:LANES] / a[:, LANES:LANES + 1]
    d = o[:tq] - lam * o[tq:]
    ms = jnp.mean(d * d, axis=-1, keepdims=True)
    o_ref[...] = (d * lax.rsqrt(ms + SUBLN_EPS) * gsub_ref[...] * (1.0 - lam_init)).astype(BF16)


def _diff_attn(z3, slopes_a, lamv, gsub, lam_init, tq, tk):
    B, S, _ = z3.shape
    assert tk % tq == 0 and S % tk == 0
    qb0, kb0, vb0 = COL_QA // LANES, COL_KA // LANES, COL_VA // LANES
    kernel = functools.partial(_diff_attn_kernel, tq=tq, tk=tk, seq=S, lam_init=lam_init)
    return pl.pallas_call(
        kernel,
        grid=(B, H_A, S // tq),
        in_specs=[
            pl.BlockSpec(memory_space=pltpu.SMEM),
            pl.BlockSpec((4, HD_A), lambda b, h, i: (0, 0)),
            pl.BlockSpec((None, tq, LANES), lambda b, h, i: (b, i, qb0 + h)),
            pl.BlockSpec((None, S, LANES), lambda b, h, i: (b, 0, kb0 + h)),
            pl.BlockSpec((None, S, LANES), lambda b, h, i: (b, 0, vb0 + h)),
            pl.BlockSpec((tk, LANES), lambda b, h, i: (0, 0)),
            pl.BlockSpec((1, LANES), lambda b, h, i: (0, 0)),
        ],
        out_specs=pl.BlockSpec((None, tq, LANES), lambda b, h, i: (b, i, h)),
        out_shape=jax.ShapeDtypeStruct((B, S, OA_W), BF16),
        scratch_shapes=[pltpu.VMEM((S, 2 * LANES), BF16), pltpu.VMEM((S, 2 * LANES), BF16),
                        pltpu.VMEM((8, LANES), F32), pltpu.VMEM((2 * tq, 2 * LANES), F32),
                        pltpu.VMEM((2 * tq, 1), F32)],
        compiler_params=pltpu.CompilerParams(
            dimension_semantics=("arbitrary", "arbitrary", "arbitrary"), vmem_limit_bytes=VMEM_LIMIT),
        name="diff_attn",
    )(slopes_a, lamv, z3, z3, z3, _key_table(tk), gsub)


def _band_attn_kernel(q_ref, kp_ref, kc_ref, kn_ref, vp_ref, vc_ref, vn_ref, o_ref, lse_ref,
                      *, length, dil, half, slopes):
    i = pl.program_id(1)
    qb = Q_BLOCK
    kw = 3 * qb
    row = lax.broadcasted_iota(jnp.int32, (qb, kw), 0)
    col = lax.broadcasted_iota(jnp.int32, (qb, kw), 1)
    rel = col - qb - row
    kpos = (i - 1) * qb + col
    valid = (jnp.abs(rel) <= half) & (kpos >= 0) & (kpos < length)
    dist = (dil * jnp.abs(rel)).astype(F32)
    nt = (((1,), (1,)), ((), ()))

    def one_class(c, carry):
        rows = pl.ds(c, qb, stride=dil) if dil > 1 else pl.ds(0, qb)
        for hh in range(H_B):
            sl = pl.ds(hh * HD_B, HD_B)
            kk = jnp.concatenate([kp_ref[c, :, sl], kc_ref[c, :, sl], kn_ref[c, :, sl]], axis=0)
            vv = jnp.concatenate([vp_ref[c, :, sl], vc_ref[c, :, sl], vn_ref[c, :, sl]], axis=0)
            s = lax.dot_general(q_ref[c, :, sl], kk, nt, preferred_element_type=F32)
            s = jnp.where(valid, s - float(slopes[hh]) * dist, NEG_INF)
            m = jnp.max(s, axis=-1, keepdims=True)
            p = jnp.exp(s - m)
            l = jnp.sum(p, axis=-1, keepdims=True)
            o_ref[hh, rows, :] = jnp.dot(p.astype(BF16), vv, preferred_element_type=F32) / l
            lse_ref[hh, rows, :] = jnp.broadcast_to(m + jnp.log(l), (qb, HD_B))
        return carry

    lax.fori_loop(0, dil, one_class, 0)


def _band_attn(qkv, g, window, dil):
    _, B, _, L, _ = qkv.shape
    S = L * dil
    half = (window // 2) // dil
    nb = L // Q_BLOCK
    slopes = _alibi_slopes(N_GROUPS_B * H_B).reshape(N_GROUPS_B, H_B)[g]
    kernel = functools.partial(_band_attn_kernel, length=L, dil=dil, half=half, slopes=slopes)

    def spec(sec, shift):
        def imap(b, i):
            return (sec, b, 0, jnp.clip(i + shift, 0, nb - 1), 0)
        return pl.BlockSpec((None, None, dil, Q_BLOCK, TN), imap)

    out_spec = pl.BlockSpec((None, H_B, Q_BLOCK * dil, HD_B), lambda b, i: (b, 0, i, 0))
    out_shape = jax.ShapeDtypeStruct((B, H_B, S, HD_B), F32)
    return pl.pallas_call(
        kernel,
        grid=(B, nb),
        in_specs=[spec(0, 0), spec(1, -1), spec(1, 0), spec(1, 1), spec(2, -1), spec(2, 0), spec(2, 1)],
        out_specs=[out_spec, out_spec],
        out_shape=[out_shape, out_shape],
        compiler_params=pltpu.CompilerParams(
            dimension_semantics=("arbitrary", "arbitrary"), vmem_limit_bytes=VMEM_LIMIT),
        name=f"band_attn_g{g}",
    )(qkv, qkv, qkv, qkv, qkv, qkv, qkv)


def _merge_kernel(x_ref, ga_ref, gb_ref, oa_ref, o0_ref, o1_ref, o2_ref, l0_ref, l1_ref, l2_ref,
                  wpa_ref, wpb_ref, wo_ref, gffn_ref, x1_ref, h2_ref):
    heads = []
    for hh in range(H_B):
        l0, l1, l2 = l0_ref[hh], l1_ref[hh], l2_ref[hh]
        m = jnp.maximum(jnp.maximum(l0, l1), l2)
        w0, w1, w2 = jnp.exp(l0 - m), jnp.exp(l1 - m), jnp.exp(l2 - m)
        heads.append(((w0 * o0_ref[hh] + w1 * o1_ref[hh] + w2 * o2_ref[hh]) / (w0 + w1 + w2)).astype(BF16))
    ob = jnp.concatenate(heads, axis=1)
    pa = jnp.dot(oa_ref[...], wpa_ref[...], preferred_element_type=F32)
    pb = jnp.dot(ob, wpb_ref[...], preferred_element_type=F32)
    merged = ga_ref[...].astype(F32) * pa + gb_ref[...].astype(F32) * pb
    x1 = x_ref[...] + jnp.dot(merged.astype(BF16), wo_ref[...], preferred_element_type=F32)
    x1_ref[...] = x1
    ms = jnp.mean(x1 * x1, axis=-1, keepdims=True)
    h2_ref[...] = (x1 * lax.rsqrt(ms + NORM_EPS) * gffn_ref[...]).astype(BF16)


def _merge(x2, z2, oa2, obs, lses, wpa, wpb, wo, gffn, seq, tm):
    T = x2.shape[0]
    tiles_per_seq = seq // tm
    row = lambda w: pl.BlockSpec((tm, w), lambda i: (i, 0))
    per_head = pl.BlockSpec((None, H_B, tm, HD_B), lambda i: (i // tiles_per_seq, 0, i % tiles_per_seq, 0))
    const = lambda a: pl.BlockSpec(a.shape, lambda i: (0, 0), pipeline_mode=pl.Buffered(1))
    return pl.pallas_call(
        _merge_kernel,
        grid=(T // tm,),
        in_specs=[row(D_MODEL),
                  pl.BlockSpec((tm, D_MODEL), lambda i: (i, COL_GATE_A // D_MODEL)),
                  pl.BlockSpec((tm, D_MODEL), lambda i: (i, COL_GATE_B // D_MODEL)),
                  row(OA_W)] + [per_head] * 6 + [const(wpa), const(wpb), const(wo), const(gffn)],
        out_specs=[row(D_MODEL), row(D_MODEL)],
        out_shape=[jax.ShapeDtypeStruct((T, D_MODEL), F32), jax.ShapeDtypeStruct((T, D_MODEL), BF16)],
        compiler_params=pltpu.CompilerParams(
            dimension_semantics=("parallel",), vmem_limit_bytes=VMEM_LIMIT),
        name="merge",
    )(x2, z2, z2, oa2, *obs, *lses, wpa, wpb, wo, gffn)


HALO = 16


def _ffn_kernel(h_ref, hp_ref, hn_ref, x1_ref, wg_ref, wv_ref, cwg_ref, cwv_ref, cbg_ref, cbv_ref,
                wd_ref, o_ref, hcat_scr, acc_scr, *, tm, tiles_per_seq, n_f):
    i = pl.program_id(0)
    f = pl.program_id(1)

    @pl.when(f == 0)
    def _():
        pos = i % tiles_per_seq
        hp = hp_ref[...]
        hn = hn_ref[...]
        hcat_scr[0:HALO, :] = jnp.where(pos == 0, jnp.zeros_like(hp), hp)
        hcat_scr[HALO:HALO + tm, :] = h_ref[...]
        hcat_scr[HALO + tm:, :] = jnp.where(pos == tiles_per_seq - 1, jnp.zeros_like(hn), hn)
        acc_scr[...] = jnp.zeros(acc_scr.shape, F32)

    hcat = hcat_scr[...]
    rows = tm + 2 * HALO

    def conv_half(w_ref, cw_ref, cb_ref):
        a = jnp.dot(hcat, w_ref[...], preferred_element_type=F32)
        a_prev = pltpu.roll(a, 1, 0)[HALO:HALO + tm]
        a_next = pltpu.roll(a, rows - 1, 0)[HALO:HALO + tm]
        cw = cw_ref[...]
        return a_prev * cw[0:1] + a[HALO:HALO + tm] * cw[1:2] + a_next * cw[2:3] + cb_ref[...]

    ug = conv_half(wg_ref, cwg_ref, cbg_ref)
    uv = conv_half(wv_ref, cwv_ref, cbv_ref)
    act = 0.5 * ug * (1.0 + lax.erf(ug * np.float32(math.sqrt(0.5)))) * uv
    acc_scr[...] += jnp.dot(act.astype(BF16), wd_ref[...], preferred_element_type=F32)

    @pl.when(f == n_f - 1)
    def _():
        o_ref[...] = x1_ref[...] + acc_scr[...]


def _ffn(h2, x1, w_up, conv_w, conv_b, w_down, seq, tm, tf):
    T = h2.shape[0]
    n_f = D_FF // tf
    hb = tm // HALO
    n_hb = T // HALO
    kernel = functools.partial(_ffn_kernel, tm=tm, tiles_per_seq=seq // tm, n_f=n_f)
    return pl.pallas_call(
        kernel,
        grid=(T // tm, n_f),
        in_specs=[
            pl.BlockSpec((tm, D_MODEL), lambda i, f: (i, 0)),
            pl.BlockSpec((HALO, D_MODEL), lambda i, f: (jnp.maximum(i * hb - 1, 0), 0)),
            pl.BlockSpec((HALO, D_MODEL), lambda i, f: (jnp.minimum((i + 1) * hb, n_hb - 1), 0)),
            pl.BlockSpec((tm, D_MODEL), lambda i, f: (i, 0)),
            pl.BlockSpec((D_MODEL, tf), lambda i, f: (0, f)),
            pl.BlockSpec((D_MODEL, tf), lambda i, f: (0, f + n_f)),
            pl.BlockSpec((3, tf), lambda i, f: (0, f)),
            pl.BlockSpec((3, tf), lambda i, f: (0, f + n_f)),
            pl.BlockSpec((1, tf), lambda i, f: (0, f)),
            pl.BlockSpec((1, tf), lambda i, f: (0, f + n_f)),
            pl.BlockSpec((tf, D_MODEL), lambda i, f: (f, 0)),
        ],
        out_specs=pl.BlockSpec((tm, D_MODEL), lambda i, f: (i, 0)),
        out_shape=jax.ShapeDtypeStruct((T, D_MODEL), F32),
        scratch_shapes=[pltpu.VMEM((tm + 2 * HALO, D_MODEL), BF16), pltpu.VMEM((tm, D_MODEL), F32)],
        compiler_params=pltpu.CompilerParams(
            dimension_semantics=("parallel", "arbitrary"), vmem_limit_bytes=VMEM_LIMIT),
        name="ffn",
    )(h2, h2, h2, x1, w_up, w_up, conv_w, conv_w, conv_b, conv_b, w_down)


def _prep_params(g_mix_norm, w_in, g_qa, g_ka, lam_q1, lam_k1, lam_q2, lam_k2, g_subln, g_qb, g_kb,
                 w_pa, w_pb, w_o, g_ffn_norm, w_up, conv_w, conv_b, w_down):
    qa, ka, va, qb, kb, vb, gate_a, gate_b = jnp.split(
        w_in, list(np.cumsum([QA_W, QA_W, QA_W, QB_W, QB_W, QB_W, D_MODEL])), axis=-1)
    w_in_p = jnp.concatenate([gate_a, gate_b, qa, ka, va, qb, kb, vb], axis=-1).astype(BF16)
    ones = lambda n: jnp.ones((n,), F32)
    gvec = jnp.concatenate([
        ones(2 * D_MODEL),
        jnp.tile(g_qa.astype(F32), 2 * H_A) * (HD_A ** -0.5 * LOG2E),
        jnp.tile(g_ka.astype(F32), 2 * H_A),
        ones(QA_W),
        jnp.tile(g_qb.astype(F32), N_GROUPS_B * H_B) * (HD_B ** -0.5),
        jnp.tile(g_kb.astype(F32), N_GROUPS_B * H_B),
        ones(QB_W)]).reshape(1, N_IN)
    idx = np.arange(TN)
    gmat = jnp.asarray(np.stack([
        (idx[:, None] // HD_A == idx[None, :] // HD_A) / HD_A,
        (idx[:, None] // HD_B == idx[None, :] // HD_B) / HD_B]).astype(np.float32), BF16)
    lamv = jnp.stack([lam_q1, lam_k1, lam_q2, lam_k2]).astype(F32)
    return dict(
        gmix=g_mix_norm.astype(F32).reshape(1, D_MODEL), w_in=w_in_p, gvec=gvec, gmat=gmat, lamv=lamv,
        gsub=g_subln.astype(F32).reshape(1, 2 * HD_A), wpa=w_pa.astype(BF16), wpb=w_pb.astype(BF16),
        wo=w_o.astype(BF16), gffn=g_ffn_norm.astype(F32).reshape(1, D_MODEL), w_up=w_up.astype(BF16),
        conv_w=conv_w.astype(F32), conv_b=conv_b.astype(F32).reshape(1, 2 * D_FF),
        w_down=w_down.astype(BF16))


def _encoder_layer(x, layer_idx, p):
    B, S, _ = x.shape
    T = B * S
    lam_init = 0.8 - 0.6 * math.exp(-0.3 * layer_idx)
    x2 = x.reshape(T, D_MODEL)
    z2, *groups = _in_proj(x2, p["gmix"], p["w_in"], p["gvec"], p["gmat"], batch=B, seq=S, tm=min(1024, S))
    z3 = z2.reshape(B, S, COL_QB)
    oa = _diff_attn(z3, jnp.asarray(_alibi_slopes(H_A)), p["lamv"], p["gsub"], lam_init,
                    tq=min(512, S // 2), tk=min(1024, S // 2))
    obs, lses = [], []
    for g, (window, dil) in enumerate(DIL_CONFIGS):
        o, lse = _band_attn(groups[g], g, window, dil)
        obs.append(o)
        lses.append(lse)
    x1, h2 = _merge(x2, z2, oa.reshape(T, OA_W), obs, lses, p["wpa"], p["wpb"], p["wo"], p["gffn"],
                    seq=S, tm=256)
    y = _ffn(h2, x1, p["w_up"], p["conv_w"], p["conv_b"], p["w_down"], seq=S, tm=512, tf=512)
    return y.reshape(B, S, D_MODEL)


def kernel(x_prompt, x_sample, g_mix_norm, w_in, g_qa, g_ka, lam_q1, lam_k1, lam_q2, lam_k2, g_subln,
           g_qb, g_kb, w_pa, w_pb, w_o, g_ffn_norm, w_up, conv_w, conv_b, w_down):
    y_prompt, y_sample = x_prompt, x_sample
    for l in range(w_in.shape[0]):
        p = _prep_params(g_mix_norm[l], w_in[l], g_qa[l], g_ka[l], lam_q1[l], lam_k1[l], lam_q2[l],
                         lam_k2[l], g_subln[l], g_qb[l], g_kb[l], w_pa[l], w_pb[l], w_o[l],
                         g_ffn_norm[l], w_up[l], conv_w[l], conv_b[l], w_down[l])
        y_prompt = _encoder_layer(y_prompt, l, p)
        y_sample = _encoder_layer(y_sample, l, p)
    return (y_prompt, y_sample)
```

```python
import functools
import math

import numpy as np
import jax
import jax.numpy as jnp
from jax import lax
from jax.experimental import pallas as pl
from jax.experimental.pallas import tpu as pltpu

D_MODEL = 2048
H_A = 8
HD_A = 64
DIL_CONFIGS = ((128, 1), (512, 4), (2048, 16))
N_GROUPS_B = 3
H_B = 4
HD_B = 128
D_FF = 5632
Q_BLOCK = 128
NORM_EPS = 1e-6
SUBLN_EPS = 1e-5
NEG_INF = -1e30

QA_W = H_A * 2 * HD_A
QB_W = N_GROUPS_B * H_B * HD_B
OA_W = QA_W
OB_W = H_B * HD_B
N_IN = 3 * QA_W + 3 * QB_W + 2 * D_MODEL

LANES = 128
COL_GATE_A = 0
COL_GATE_B = D_MODEL
COL_QA = 2 * D_MODEL
COL_KA = COL_QA + QA_W
COL_VA = COL_KA + QA_W
COL_QB = COL_VA + QA_W
COL_KB = COL_QB + QB_W
COL_VB = COL_KB + QB_W
TN = 512
N_TILES = N_IN // TN
VMEM_LIMIT = 56 * 1024 * 1024

F32 = jnp.float32
BF16 = jnp.bfloat16


def _alibi_slopes(n):
    return np.asarray(2.0 ** (-8.0 * (np.arange(n) + 1) / n), dtype=np.float32)


T_QA, T_VA, T_QB, T_KB, T_VB = COL_QA // TN, COL_VA // TN, COL_QB // TN, COL_KB // TN, COL_VB // TN
DILS = tuple(d for _, d in DIL_CONFIGS)


def _in_proj_kernel(x_ref, gmix_ref, w_ref, gvec_ref, gmat_ref, z_ref, g0_ref, g1_ref, g2_ref,
                    h_scr, t_scr, *, tm):
    j = pl.program_id(1)

    @pl.when(j == 0)
    def _():
        x = x_ref[...]
        ms = jnp.mean(x * x, axis=-1, keepdims=True)
        h_scr[...] = (x * lax.rsqrt(ms + NORM_EPS) * gmix_ref[...]).astype(BF16)

    z = jnp.dot(h_scr[...], w_ref[...], preferred_element_type=F32)

    def normed():
        ms = jnp.dot((z * z).astype(BF16), gmat_ref[...], preferred_element_type=F32)
        return z * lax.rsqrt(ms + NORM_EPS) * gvec_ref[...]

    @pl.when(j < T_QA)
    def _():
        z_ref[...] = jax.nn.sigmoid(z).astype(BF16)

    @pl.when((j >= T_QA) & (j < T_VA))
    def _():
        z_ref[...] = normed().astype(BF16)

    @pl.when((j >= T_VA) & (j < T_QB))
    def _():
        z_ref[...] = z.astype(BF16)

    def scatter(g_ref, dil, val):
        if dil == 1:
            g_ref[0] = val.astype(BF16)
        else:
            for kb in range(TN // LANES):
                t_scr[kb] = val[:, kb * LANES:(kb + 1) * LANES]
            for c in range(dil):
                for kb in range(TN // LANES):
                    g_ref[c, :, kb * LANES:(kb + 1) * LANES] = (
                        t_scr[kb, pl.ds(c, tm // dil, stride=dil), :].astype(BF16))

    for g, (g_ref, dil) in enumerate(zip((g0_ref, g1_ref, g2_ref), DILS)):
        @pl.when((j == T_QB + g) | (j == T_KB + g))
        def _(g_ref=g_ref, dil=dil):
            scatter(g_ref, dil, normed())

        @pl.when(j == T_VB + g)
        def _(g_ref=g_ref, dil=dil):
            scatter(g_ref, dil, z)


def _in_proj(x2, gmix, w_in_p, gvec, gmat, batch, seq, tm):
    T = x2.shape[0]
    tiles_per_seq = seq // tm
    main_w = COL_QB

    def group_spec(g, dil):
        def imap(i, j):
            sec = (j >= T_KB + g).astype(jnp.int32) + (j >= T_VB + g).astype(jnp.int32)
            return (sec, i // tiles_per_seq, 0, i % tiles_per_seq, 0)
        return pl.BlockSpec((None, None, dil, tm // dil, TN), imap)

    return pl.pallas_call(
        functools.partial(_in_proj_kernel, tm=tm),
        grid=(T // tm, N_TILES),
        in_specs=[
            pl.BlockSpec((tm, D_MODEL), lambda i, j: (i, 0)),
            pl.BlockSpec((1, D_MODEL), lambda i, j: (0, 0)),
            pl.BlockSpec((D_MODEL, TN), lambda i, j: (0, j)),
            pl.BlockSpec((1, TN), lambda i, j: (0, j)),
            pl.BlockSpec((None, TN, TN), lambda i, j: (jnp.where(j < T_QB, 0, 1), 0, 0)),
        ],
        out_specs=[pl.BlockSpec((tm, TN), lambda i, j: (i, jnp.minimum(j, T_QB - 1)))]
                  + [group_spec(g, dil) for g, dil in enumerate(DILS)],
        out_shape=[jax.ShapeDtypeStruct((T, main_w), BF16)]
                  + [jax.ShapeDtypeStruct((3, batch, dil, seq // dil, TN), BF16) for dil in DILS],
        scratch_shapes=[pltpu.VMEM((tm, D_MODEL), BF16), pltpu.VMEM((TN // LANES, tm, LANES), F32)],
        compiler_params=pltpu.CompilerParams(
            dimension_semantics=("arbitrary", "arbitrary"), vmem_limit_bytes=VMEM_LIMIT),
        name="in_proj",
    )(x2, gmix, w_in_p, gvec, gmat)


LOG2E = math.log2(math.e)
AUG_P, AUG_J, AUG_R = 0, 3, 6
REF_HEADROOM = 60.0
BOUND_LIMIT = 80.0


def _split3(x):
    def top(a):
        bits = lax.bitcast_convert_type(a, jnp.uint32) & jnp.uint32(0xFFFF0000)
        return lax.bitcast_convert_type(bits, F32)
    hi = top(x)
    mid = top(x - hi)
    lo = top(x - hi - mid)
    return hi, mid, lo


def _key_table(tk):
    def top(a):
        return (a.view(np.uint32) & np.uint32(0xFFFF0000)).view(np.float32)
    x = (np.arange(tk, dtype=np.float32) * np.float32(LOG2E)).astype(np.float32)
    hi = top(x)
    mid = top(x - hi)
    lo = top(x - hi - mid)
    tab = np.zeros((tk, LANES), np.float32)
    tab[:, AUG_P:AUG_P + 3] = 1.0
    tab[:, AUG_J], tab[:, AUG_J + 1], tab[:, AUG_J + 2] = hi, mid, lo
    tab[:, AUG_R] = 1.0
    return jnp.asarray(tab).astype(BF16)


def _diff_attn_kernel(slopes_ref, lamv_ref, q_ref, k_ref, v_ref, ktab_ref, gsub_ref, o_ref,
                      kcat, vcat, kstat, acc, m_scr, *, tq, tk, seq, lam_init):
    h = pl.program_id(1)
    i = pl.program_id(2)
    n_chunks = seq // tk
    slope = slopes_ref[h]
    cs = slope * LOG2E
    nt = (((1,), (1,)), ((), ()))

    @pl.when(i == 0)
    def _():
        lane_k = lax.broadcasted_iota(jnp.int32, (tk, LANES), 1)
        ones_col = jnp.where(lane_k == 0, 1.0, 0.0).astype(BF16)

        def fill(c, carry):
            n1, n2 = carry
            sl = pl.ds(pl.multiple_of(c * tk, tk), tk)
            kc = k_ref[sl, :]
            kcat[sl, 0:LANES] = kc
            kcat[sl, LANES:2 * LANES] = ktab_ref[...]
            vcat[sl, 0:LANES] = v_ref[sl, :]
            vcat[sl, LANES:2 * LANES] = ones_col
            kf = kc.astype(F32)
            kk = kf * kf
            s1 = jnp.sum(jnp.where(lane_k < HD_A, kk, 0.0), axis=-1, keepdims=True)
            s2 = jnp.sum(jnp.where(lane_k >= HD_A, kk, 0.0), axis=-1, keepdims=True)
            return jnp.maximum(n1, s1), jnp.maximum(n2, s2)

        z = jnp.zeros((tk, 1), F32)
        n1, n2 = lax.fori_loop(0, n_chunks, fill, (z, z))
        kstat[0:1, :] = jnp.broadcast_to(jnp.sqrt(jnp.max(n1, axis=0, keepdims=True)), (1, LANES))
        kstat[1:2, :] = jnp.broadcast_to(jnp.sqrt(jnp.max(n2, axis=0, keepdims=True)), (1, LANES))

    q = q_ref[...]
    lane = lax.broadcasted_iota(jnp.int32, (tq, LANES), 1)
    qf = q.astype(F32)
    qq = qf * qf
    qn1 = jnp.sqrt(jnp.sum(jnp.where(lane < HD_A, qq, 0.0), axis=-1, keepdims=True))
    qn2 = jnp.sqrt(jnp.sum(jnp.where(lane >= HD_A, qq, 0.0), axis=-1, keepdims=True))
    bound = jnp.concatenate([qn1 * kstat[0:1, 0:1], qn2 * kstat[1:2, 0:1]], axis=0) * 1.01
    bound_ok = jnp.max(bound) <= BOUND_LIMIT

    zero = jnp.zeros_like(q)
    base = jnp.concatenate([jnp.where(lane < HD_A, q, zero), jnp.where(lane >= HD_A, q, zero)], axis=0)
    row2 = lax.broadcasted_iota(jnp.int32, (2 * tq, LANES), 0)
    lane2 = lax.broadcasted_iota(jnp.int32, (2 * tq, LANES), 1)
    iq = jnp.where(row2 >= tq, row2 - tq, row2).astype(F32)
    hi, mid, lo = _split3(cs * iq)
    ppos = jnp.where(lane2 == AUG_P, hi, jnp.where(lane2 == AUG_P + 1, mid,
                                                   jnp.where(lane2 == AUG_P + 2, lo, 0.0)))
    pslope = jnp.where((lane2 >= AUG_J) & (lane2 < AUG_J + 3), slope, 0.0)
    rcol = jnp.where(lane2 == AUG_R, REF_HEADROOM - bound, 0.0)
    q_left = jnp.concatenate([base, (rcol - ppos + pslope).astype(BF16)], axis=1)
    q_right = jnp.concatenate([base, (rcol + ppos - pslope).astype(BF16)], axis=1)
    q_diag = jnp.concatenate([base, rcol.astype(BF16)], axis=1)
    q_plain = jnp.concatenate([base, jnp.zeros_like(base)], axis=1)

    row = lax.broadcasted_iota(jnp.int32, (2 * tq, tq), 0)
    col = lax.broadcasted_iota(jnp.int32, (2 * tq, tq), 1)
    rel_own = (jnp.where(row >= tq, row - tq, row) - col).astype(F32)
    i0 = i * tq
    cd = i0 // tk

    def chunk(c):
        return pl.ds(pl.multiple_of(c * tk, tk), tk)

    def fixed_reference():
        acc[...] = jnp.zeros(acc.shape, F32)

        def folded(keys, left, base):
            off = cs * (i0 - base).astype(F32)
            s = lax.dot_general(jnp.where(left, q_left, q_right), kcat[keys, :], nt,
                                preferred_element_type=F32) + jnp.where(left, -off, off)
            acc[...] += jnp.dot(jnp.exp2(s).astype(BF16), vcat[keys, :], preferred_element_type=F32)

        def body(c, carry):
            cc = c + (c >= cd).astype(jnp.int32)
            folded(chunk(cc), cc < cd, cc * tk)
            return carry

        n_main = n_chunks - 1
        lax.fori_loop(0, n_main, body, 0, unroll=3 if n_main % 3 == 0 else 2 if n_main % 2 == 0 else 1)

        own = (i0 - cd * tk) // tq
        s = lax.dot_general(q_diag, kcat[pl.ds(pl.multiple_of(i0, tq), tq), :], nt, preferred_element_type=F32)
        s = s - cs * jnp.abs(rel_own)
        acc[...] += jnp.dot(jnp.exp2(s).astype(BF16), vcat[pl.ds(pl.multiple_of(i0, tq), tq), :],
                            preferred_element_type=F32)
        for k in range(1, tk // tq):
            blk = (own + k) % (tk // tq)
            keys = pl.ds(pl.multiple_of(cd * tk + blk * tq, tq), tq)
            folded(keys, blk < own, cd * tk)

    def online_reference():
        acc[...] = jnp.zeros(acc.shape, F32)
        m_scr[...] = jnp.full(m_scr.shape, NEG_INF, F32)

        def body(c, carry):
            keys = pl.ds(pl.multiple_of(c * tq, tq), tq)
            s = lax.dot_general(q_plain, kcat[keys, :], nt, preferred_element_type=F32)
            s = s - cs * jnp.abs(rel_own + (i0 - c * tq).astype(F32))
            m_prev = m_scr[...]
            m_new = jnp.maximum(m_prev, jnp.max(s, axis=-1, keepdims=True))
            p = jnp.exp2(s - m_new)
            acc[...] = jnp.exp2(m_prev - m_new) * acc[...] + jnp.dot(
                p.astype(BF16), vcat[keys, :], preferred_element_type=F32)
            m_scr[...] = m_new
            return carry

        lax.fori_loop(0, seq // tq, body, 0)

    lax.cond(bound_ok, fixed_reference, online_reference)

    lamv = lamv_ref[...]
    lam = (jnp.exp(jnp.sum(lamv[0:1] * lamv[1:2], axis=-1, keepdims=True))
           - jnp.exp(jnp.sum(lamv[2:3] * lamv[3:4], axis=-1, keepdims=True)) + lam_init)
    a = acc[...]
    o = a[:, 0:LANES] / a[:, LANES:LANES + 1]
    d = o[:tq] - lam * o[tq:]
    ms = jnp.mean(d * d, axis=-1, keepdims=True)
    o_ref[...] = (d * lax.rsqrt(ms + SUBLN_EPS) * gsub_ref[...] * (1.0 - lam_init)).astype(BF16)


def _diff_attn(z3, slopes_a, lamv, gsub, lam_init, tq, tk):
    B, S, _ = z3.shape
    assert tk % tq == 0 and S % tk == 0
    qb0, kb0, vb0 = COL_QA // LANES, COL_KA // LANES, COL_VA // LANES
    kernel = functools.partial(_diff_attn_kernel, tq=tq, tk=tk, seq=S, lam_init=lam_init)
    return pl.pallas_call(
        kernel,
        grid=(B, H_A, S // tq),
        in_specs=[
            pl.BlockSpec(memory_space=pltpu.SMEM),
            pl.BlockSpec((4, HD_A), lambda b, h, i: (0, 0)),
            pl.BlockSpec((None, tq, LANES), lambda b, h, i: (b, i, qb0 + h)),
            pl.BlockSpec((None, S, LANES), lambda b, h, i: (b, 0, kb0 + h)),
            pl.BlockSpec((None, S, LANES), lambda b, h, i: (b, 0, vb0 + h)),
            pl.BlockSpec((tk, LANES), lambda b, h, i: (0, 0)),
            pl.BlockSpec((1, LANES), lambda b, h, i: (0, 0)),
        ],
        out_specs=pl.BlockSpec((None, tq, LANES), lambda b, h, i: (b, i, h)),
        out_shape=jax.ShapeDtypeStruct((B, S, OA_W), BF16),
        scratch_shapes=[pltpu.VMEM((S, 2 * LANES), BF16), pltpu.VMEM((S, 2 * LANES), BF16),
                        pltpu.VMEM((8, LANES), F32), pltpu.VMEM((2 * tq, 2 * LANES), F32),
                        pltpu.VMEM((2 * tq, 1), F32)],
        compiler_params=pltpu.CompilerParams(
            dimension_semantics=("arbitrary", "arbitrary", "arbitrary"), vmem_limit_bytes=VMEM_LIMIT),
        name="diff_attn",
    )(slopes_a, lamv, z3, z3, z3, _key_table(tk), gsub)


def _band_attn_kernel(q_ref, kp_ref, kc_ref, kn_ref, vp_ref, vc_ref, vn_ref, o_ref, lse_ref,
                      *, length, dil, half, slopes):
    i = pl.program_id(1)
    qb = Q_BLOCK
    kw = 3 * qb
    row = lax.broadcasted_iota(jnp.int32, (qb, kw), 0)
    col = lax.broadcasted_iota(jnp.int32, (qb, kw), 1)
    rel = col - qb - row
    kpos = (i - 1) * qb + col
    valid = (jnp.abs(rel) <= half) & (kpos >= 0) & (kpos < length)
    dist = (dil * jnp.abs(rel)).astype(F32)
    nt = (((1,), (1,)), ((), ()))

    def one_class(c, carry):
        rows = pl.ds(c, qb, stride=dil) if dil > 1 else pl.ds(0, qb)
        for hh in range(H_B):
            sl = pl.ds(hh * HD_B, HD_B)
            kk = jnp.concatenate([kp_ref[c, :, sl], kc_ref[c, :, sl], kn_ref[c, :, sl]], axis=0)
            vv = jnp.concatenate([vp_ref[c, :, sl], vc_ref[c, :, sl], vn_ref[c, :, sl]], axis=0)
            s = lax.dot_general(q_ref[c, :, sl], kk, nt, preferred_element_type=F32)
            s = jnp.where(valid, s - float(slopes[hh]) * dist, NEG_INF)
            m = jnp.max(s, axis=-1, keepdims=True)
            p = jnp.exp(s - m)
            l = jnp.sum(p, axis=-1, keepdims=True)
            o_ref[hh, rows, :] = jnp.dot(p.astype(BF16), vv, preferred_element_type=F32) / l
            lse_ref[hh, rows, :] = jnp.broadcast_to(m + jnp.log(l), (qb, HD_B))
        return carry

    lax.fori_loop(0, dil, one_class, 0, unroll=2 if dil % 2 == 0 else 1)


def _band_attn(qkv, g, window, dil):
    _, B, _, L, _ = qkv.shape
    S = L * dil
    half = (window // 2) // dil
    nb = L // Q_BLOCK
    slopes = _alibi_slopes(N_GROUPS_B * H_B).reshape(N_GROUPS_B, H_B)[g]
    kernel = functools.partial(_band_attn_kernel, length=L, dil=dil, half=half, slopes=slopes)

    def spec(sec, shift):
        def imap(b, i):
            return (sec, b, 0, jnp.clip(i + shift, 0, nb - 1), 0)
        return pl.BlockSpec((None, None, dil, Q_BLOCK, TN), imap)

    out_spec = pl.BlockSpec((None, H_B, Q_BLOCK * dil, HD_B), lambda b, i: (b, 0, i, 0))
    out_shape = jax.ShapeDtypeStruct((B, H_B, S, HD_B), F32)
    return pl.pallas_call(
        kernel,
        grid=(B, nb),
        in_specs=[spec(0, 0), spec(1, -1), spec(1, 0), spec(1, 1), spec(2, -1), spec(2, 0), spec(2, 1)],
        out_specs=[out_spec, out_spec],
        out_shape=[out_shape, out_shape],
        compiler_params=pltpu.CompilerParams(
            dimension_semantics=("arbitrary", "arbitrary"), vmem_limit_bytes=VMEM_LIMIT),
        name=f"band_attn_g{g}",
    )(qkv, qkv, qkv, qkv, qkv, qkv, qkv)


def _merge_kernel(x_ref, ga_ref, gb_ref, oa_ref, o0_ref, o1_ref, o2_ref, l0_ref, l1_ref, l2_ref,
                  wpa_ref, wpb_ref, wo_ref, gffn_ref, x1_ref, h2_ref):
    heads = []
    for hh in range(H_B):
        l0, l1, l2 = l0_ref[hh], l1_ref[hh], l2_ref[hh]
        m = jnp.maximum(jnp.maximum(l0, l1), l2)
        w0, w1, w2 = jnp.exp(l0 - m), jnp.exp(l1 - m), jnp.exp(l2 - m)
        heads.append(((w0 * o0_ref[hh] + w1 * o1_ref[hh] + w2 * o2_ref[hh]) / (w0 + w1 + w2)).astype(BF16))
    ob = jnp.concatenate(heads, axis=1)
    pa = jnp.dot(oa_ref[...], wpa_ref[...], preferred_element_type=F32)
    pb = jnp.dot(ob, wpb_ref[...], preferred_element_type=F32)
    merged = ga_ref[...].astype(F32) * pa + gb_ref[...].astype(F32) * pb
    x1 = x_ref[...] + jnp.dot(merged.astype(BF16), wo_ref[...], preferred_element_type=F32)
    x1_ref[...] = x1
    ms = jnp.mean(x1 * x1, axis=-1, keepdims=True)
    h2_ref[...] = (x1 * lax.rsqrt(ms + NORM_EPS) * gffn_ref[...]).astype(BF16)


def _merge(x2, z2, oa2, obs, lses, wpa, wpb, wo, gffn, seq, tm):
    T = x2.shape[0]
    tiles_per_seq = seq // tm
    row = lambda w: pl.BlockSpec((tm, w), lambda i: (i, 0))
    per_head = pl.BlockSpec((None, H_B, tm, HD_B), lambda i: (i // tiles_per_seq, 0, i % tiles_per_seq, 0))
    const = lambda a: pl.BlockSpec(a.shape, lambda i: (0, 0), pipeline_mode=pl.Buffered(1))
    return pl.pallas_call(
        _merge_kernel,
        grid=(T // tm,),
        in_specs=[row(D_MODEL),
                  pl.BlockSpec((tm, D_MODEL), lambda i: (i, COL_GATE_A // D_MODEL)),
                  pl.BlockSpec((tm, D_MODEL), lambda i: (i, COL_GATE_B // D_MODEL)),
                  row(OA_W)] + [per_head] * 6 + [const(wpa), const(wpb), const(wo), const(gffn)],
        out_specs=[row(D_MODEL), row(D_MODEL)],
        out_shape=[jax.ShapeDtypeStruct((T, D_MODEL), F32), jax.ShapeDtypeStruct((T, D_MODEL), BF16)],
        compiler_params=pltpu.CompilerParams(
            dimension_semantics=("parallel",), vmem_limit_bytes=VMEM_LIMIT),
        name="merge",
    )(x2, z2, z2, oa2, *obs, *lses, wpa, wpb, wo, gffn)


HALO = 16


def _ffn_kernel(h_ref, hp_ref, hn_ref, x1_ref, wg_ref, wv_ref, cwg_ref, cwv_ref, cbg_ref, cbv_ref,
                wd_ref, o_ref, hcat_scr, acc_scr, *, tm, tiles_per_seq, n_f):
    i = pl.program_id(0)
    f = pl.program_id(1)

    @pl.when(f == 0)
    def _():
        pos = i % tiles_per_seq
        hp = hp_ref[...]
        hn = hn_ref[...]
        hcat_scr[0:HALO, :] = jnp.where(pos == 0, jnp.zeros_like(hp), hp)
        hcat_scr[HALO:HALO + tm, :] = h_ref[...]
        hcat_scr[HALO + tm:, :] = jnp.where(pos == tiles_per_seq - 1, jnp.zeros_like(hn), hn)
        acc_scr[...] = jnp.zeros(acc_scr.shape, F32)

    hcat = hcat_scr[...]
    rows = tm + 2 * HALO

    def conv_half(w_ref, cw_ref, cb_ref):
        a = jnp.dot(hcat, w_ref[...], preferred_element_type=F32)
        a_prev = pltpu.roll(a, 1, 0)[HALO:HALO + tm]
        a_next = pltpu.roll(a, rows - 1, 0)[HALO:HALO + tm]
        cw = cw_ref[...]
        return a_prev * cw[0:1] + a[HALO:HALO + tm] * cw[1:2] + a_next * cw[2:3] + cb_ref[...]

    ug = conv_half(wg_ref, cwg_ref, cbg_ref)
    uv = conv_half(wv_ref, cwv_ref, cbv_ref)
    act = 0.5 * ug * (1.0 + lax.erf(ug * np.float32(math.sqrt(0.5)))) * uv
    acc_scr[...] += jnp.dot(act.astype(BF16), wd_ref[...], preferred_element_type=F32)

    @pl.when(f == n_f - 1)
    def _():
        o_ref[...] = x1_ref[...] + acc_scr[...]


def _ffn(h2, x1, w_up, conv_w, conv_b, w_down, seq, tm, tf):
    T = h2.shape[0]
    n_f = D_FF // tf
    hb = tm // HALO
    n_hb = T // HALO
    kernel = functools.partial(_ffn_kernel, tm=tm, tiles_per_seq=seq // tm, n_f=n_f)
    return pl.pallas_call(
        kernel,
        grid=(T // tm, n_f),
        in_specs=[
            pl.BlockSpec((tm, D_MODEL), lambda i, f: (i, 0)),
            pl.BlockSpec((HALO, D_MODEL), lambda i, f: (jnp.maximum(i * hb - 1, 0), 0)),
            pl.BlockSpec((HALO, D_MODEL), lambda i, f: (jnp.minimum((i + 1) * hb, n_hb - 1), 0)),
            pl.BlockSpec((tm, D_MODEL), lambda i, f: (i, 0)),
            pl.BlockSpec((D_MODEL, tf), lambda i, f: (0, f)),
            pl.BlockSpec((D_MODEL, tf), lambda i, f: (0, f + n_f)),
            pl.BlockSpec((3, tf), lambda i, f: (0, f)),
            pl.BlockSpec((3, tf), lambda i, f: (0, f + n_f)),
            pl.BlockSpec((1, tf), lambda i, f: (0, f)),
            pl.BlockSpec((1, tf), lambda i, f: (0, f + n_f)),
            pl.BlockSpec((tf, D_MODEL), lambda i, f: (f, 0)),
        ],
        out_specs=pl.BlockSpec((tm, D_MODEL), lambda i, f: (i, 0)),
        out_shape=jax.ShapeDtypeStruct((T, D_MODEL), F32),
        scratch_shapes=[pltpu.VMEM((tm + 2 * HALO, D_MODEL), BF16), pltpu.VMEM((tm, D_MODEL), F32)],
        compiler_params=pltpu.CompilerParams(
            dimension_semantics=("parallel", "arbitrary"), vmem_limit_bytes=VMEM_LIMIT),
        name="ffn",
    )(h2, h2, h2, x1, w_up, w_up, conv_w, conv_w, conv_b, conv_b, w_down)


def _prep_params(g_mix_norm, w_in, g_qa, g_ka, lam_q1, lam_k1, lam_q2, lam_k2, g_subln, g_qb, g_kb,
                 w_pa, w_pb, w_o, g_ffn_norm, w_up, conv_w, conv_b, w_down):
    qa, ka, va, qb, kb, vb, gate_a, gate_b = jnp.split(
        w_in, list(np.cumsum([QA_W, QA_W, QA_W, QB_W, QB_W, QB_W, D_MODEL])), axis=-1)
    w_in_p = jnp.concatenate([gate_a, gate_b, qa, ka, va, qb, kb, vb], axis=-1).astype(BF16)
    ones = lambda n: jnp.ones((n,), F32)
    gvec = jnp.concatenate([
        ones(2 * D_MODEL),
        jnp.tile(g_qa.astype(F32), 2 * H_A) * (HD_A ** -0.5 * LOG2E),
        jnp.tile(g_ka.astype(F32), 2 * H_A),
        ones(QA_W),
        jnp.tile(g_qb.astype(F32), N_GROUPS_B * H_B) * (HD_B ** -0.5),
        jnp.tile(g_kb.astype(F32), N_GROUPS_B * H_B),
        ones(QB_W)]).reshape(1, N_IN)
    idx = np.arange(TN)
    gmat = jnp.asarray(np.stack([
        (idx[:, None] // HD_A == idx[None, :] // HD_A) / HD_A,
        (idx[:, None] // HD_B == idx[None, :] // HD_B) / HD_B]).astype(np.float32), BF16)
    lamv = jnp.stack([lam_q1, lam_k1, lam_q2, lam_k2]).astype(F32)
    return dict(
        gmix=g_mix_norm.astype(F32).reshape(1, D_MODEL), w_in=w_in_p, gvec=gvec, gmat=gmat, lamv=lamv,
        gsub=g_subln.astype(F32).reshape(1, 2 * HD_A), wpa=w_pa.astype(BF16), wpb=w_pb.astype(BF16),
        wo=w_o.astype(BF16), gffn=g_ffn_norm.astype(F32).reshape(1, D_MODEL), w_up=w_up.astype(BF16),
        conv_w=conv_w.astype(F32), conv_b=conv_b.astype(F32).reshape(1, 2 * D_FF),
        w_down=w_down.astype(BF16))


def _encoder_layer(x, layer_idx, p):
    B, S, _ = x.shape
    T = B * S
    lam_init = 0.8 - 0.6 * math.exp(-0.3 * layer_idx)
    x2 = x.reshape(T, D_MODEL)
    z2, *groups = _in_proj(x2, p["gmix"], p["w_in"], p["gvec"], p["gmat"], batch=B, seq=S, tm=min(1024, S))
    z3 = z2.reshape(B, S, COL_QB)
    oa = _diff_attn(z3, jnp.asarray(_alibi_slopes(H_A)), p["lamv"], p["gsub"], lam_init,
                    tq=min(512, S // 2), tk=min(1024, S // 2))
    obs, lses = [], []
    for g, (window, dil) in enumerate(DIL_CONFIGS):
        o, lse = _band_attn(groups[g], g, window, dil)
        obs.append(o)
        lses.append(lse)
    x1, h2 = _merge(x2, z2, oa.reshape(T, OA_W), obs, lses, p["wpa"], p["wpb"], p["wo"], p["gffn"],
                    seq=S, tm=256)
    y = _ffn(h2, x1, p["w_up"], p["conv_w"], p["conv_b"], p["w_down"], seq=S, tm=512, tf=512)
    return y.reshape(B, S, D_MODEL)


def kernel(x_prompt, x_sample, g_mix_norm, w_in, g_qa, g_ka, lam_q1, lam_k1, lam_q2, lam_k2, g_subln,
           g_qb, g_kb, w_pa, w_pb, w_o, g_ffn_norm, w_up, conv_w, conv_b, w_down):
    y_prompt, y_sample = x_prompt, x_sample
    for l in range(w_in.shape[0]):
        p = _prep_params(g_mix_norm[l], w_in[l], g_qa[l], g_ka[l], lam_q1[l], lam_k1[l], lam_q2[l],
                         lam_k2[l], g_subln[l], g_qb[l], g_kb[l], w_pa[l], w_pb[l], w_o[l],
                         g_ffn_norm[l], w_up[l], conv_w[l], conv_b[l], w_down[l])
        y_prompt = _encoder_layer(y_prompt, l, p)
        y_sample = _encoder_layer(y_sample, l, p)
    return (y_prompt, y_sample)
```

```python
import functools
import math

import numpy as np
import jax
import jax.numpy as jnp
from jax import lax
from jax.experimental import pallas as pl
from jax.experimental.pallas import tpu as pltpu

D_MODEL = 2048
H_A = 8
HD_A = 64
DIL_CONFIGS = ((128, 1), (512, 4), (2048, 16))
N_GROUPS_B = 3
H_B = 4
HD_B = 128
D_FF = 5632
Q_BLOCK = 128
NORM_EPS = 1e-6
SUBLN_EPS = 1e-5
NEG_INF = -1e30

QA_W = H_A * 2 * HD_A
QB_W = N_GROUPS_B * H_B * HD_B
OA_W = QA_W
OB_W = H_B * HD_B
N_IN = 3 * QA_W + 3 * QB_W + 2 * D_MODEL

LANES = 128
COL_GATE_A = 0
COL_GATE_B = D_MODEL
COL_QA = 2 * D_MODEL
COL_KA = COL_QA + QA_W
COL_VA = COL_KA + QA_W
COL_QB = COL_VA + QA_W
COL_KB = COL_QB + QB_W
COL_VB = COL_KB + QB_W
TN = 512
N_TILES = N_IN // TN
VMEM_LIMIT = 56 * 1024 * 1024

F32 = jnp.float32
BF16 = jnp.bfloat16


def _alibi_slopes(n):
    return np.asarray(2.0 ** (-8.0 * (np.arange(n) + 1) / n), dtype=np.float32)


T_QA, T_VA, T_QB, T_KB, T_VB = COL_QA // TN, COL_VA // TN, COL_QB // TN, COL_KB // TN, COL_VB // TN
DILS = tuple(d for _, d in DIL_CONFIGS)


def _in_proj_kernel(x_ref, gmix_ref, w_ref, gvec_ref, gmat_ref, z_ref, g0_ref, g1_ref, g2_ref,
                    h_scr, t_scr, *, tm):
    j = pl.program_id(1)

    @pl.when(j == 0)
    def _():
        x = x_ref[...]
        ms = jnp.mean(x * x, axis=-1, keepdims=True)
        h_scr[...] = (x * lax.rsqrt(ms + NORM_EPS) * gmix_ref[...]).astype(BF16)

    def plain():
        return jnp.dot(h_scr[...], w_ref[...], preferred_element_type=F32)

    def normed():
        z = plain()
        ms = jnp.dot((z * z).astype(BF16), gmat_ref[...], preferred_element_type=F32)
        return z * lax.rsqrt(ms + NORM_EPS) * gvec_ref[...]

    @pl.when((j >= T_QA) & (j < T_VA))
    def _():
        z_ref[...] = normed().astype(BF16)

    @pl.when((j < T_QA) | ((j >= T_VA) & (j < T_QB)))
    def _():
        z_ref[...] = plain().astype(BF16)

    def scatter(g_ref, dil, val):
        if dil == 1:
            g_ref[0] = val.astype(BF16)
        else:
            for kb in range(TN // LANES):
                t_scr[kb] = val[:, kb * LANES:(kb + 1) * LANES]
            for c in range(dil):
                for kb in range(TN // LANES):
                    g_ref[c, :, kb * LANES:(kb + 1) * LANES] = (
                        t_scr[kb, pl.ds(c, tm // dil, stride=dil), :].astype(BF16))

    for g, (g_ref, dil) in enumerate(zip((g0_ref, g1_ref, g2_ref), DILS)):
        @pl.when((j == T_QB + g) | (j == T_KB + g))
        def _(g_ref=g_ref, dil=dil):
            scatter(g_ref, dil, normed())

        @pl.when(j == T_VB + g)
        def _(g_ref=g_ref, dil=dil):
            scatter(g_ref, dil, plain())


def _in_proj(x2, gmix, w_in_p, gvec, gmat, batch, seq, tm):
    T = x2.shape[0]
    tiles_per_seq = seq // tm
    main_w = COL_QB

    def group_spec(g, dil):
        def imap(i, j):
            sec = (j >= T_KB + g).astype(jnp.int32) + (j >= T_VB + g).astype(jnp.int32)
            return (sec, i // tiles_per_seq, 0, i % tiles_per_seq, 0)
        return pl.BlockSpec((None, None, dil, tm // dil, TN), imap)

    return pl.pallas_call(
        functools.partial(_in_proj_kernel, tm=tm),
        grid=(T // tm, N_TILES),
        in_specs=[
            pl.BlockSpec((tm, D_MODEL), lambda i, j: (i, 0)),
            pl.BlockSpec((1, D_MODEL), lambda i, j: (0, 0)),
            pl.BlockSpec((D_MODEL, TN), lambda i, j: (0, j)),
            pl.BlockSpec((1, TN), lambda i, j: (0, j)),
            pl.BlockSpec((None, TN, TN), lambda i, j: (jnp.where(j < T_QB, 0, 1), 0, 0)),
        ],
        out_specs=[pl.BlockSpec((tm, TN), lambda i, j: (i, jnp.minimum(j, T_QB - 1)))]
                  + [group_spec(g, dil) for g, dil in enumerate(DILS)],
        out_shape=[jax.ShapeDtypeStruct((T, main_w), BF16)]
                  + [jax.ShapeDtypeStruct((3, batch, dil, seq // dil, TN), BF16) for dil in DILS],
        scratch_shapes=[pltpu.VMEM((tm, D_MODEL), BF16), pltpu.VMEM((TN // LANES, tm, LANES), F32)],
        compiler_params=pltpu.CompilerParams(
            dimension_semantics=("arbitrary", "arbitrary"), vmem_limit_bytes=VMEM_LIMIT),
        name="in_proj",
    )(x2, gmix, w_in_p, gvec, gmat)


LOG2E = math.log2(math.e)
AUG_P, AUG_J, AUG_R = 0, 3, 6
REF_HEADROOM = 60.0
BOUND_LIMIT = 80.0


def _split3(x):
    def top(a):
        bits = lax.bitcast_convert_type(a, jnp.uint32) & jnp.uint32(0xFFFF0000)
        return lax.bitcast_convert_type(bits, F32)
    hi = top(x)
    mid = top(x - hi)
    lo = top(x - hi - mid)
    return hi, mid, lo


def _key_table(tk):
    def top(a):
        return (a.view(np.uint32) & np.uint32(0xFFFF0000)).view(np.float32)
    x = (np.arange(tk, dtype=np.float32) * np.float32(LOG2E)).astype(np.float32)
    hi = top(x)
    mid = top(x - hi)
    lo = top(x - hi - mid)
    tab = np.zeros((tk, LANES), np.float32)
    tab[:, AUG_P:AUG_P + 3] = 1.0
    tab[:, AUG_J], tab[:, AUG_J + 1], tab[:, AUG_J + 2] = hi, mid, lo
    tab[:, AUG_R] = 1.0
    return jnp.asarray(tab).astype(BF16)


def _diff_attn_kernel(slopes_ref, lamv_ref, q_ref, k_ref, v_ref, ktab_ref, gsub_ref, o_ref,
                      kcat, vcat, kstat, acc, m_scr, *, tq, tk, seq, lam_init):
    h = pl.program_id(1)
    i = pl.program_id(2)
    n_chunks = seq // tk
    slope = slopes_ref[h]
    cs = slope * LOG2E
    nt = (((1,), (1,)), ((), ()))

    @pl.when(i == 0)
    def _():
        lane_k = lax.broadcasted_iota(jnp.int32, (tk, LANES), 1)
        ones_col = jnp.where(lane_k == 0, 1.0, 0.0).astype(BF16)

        def fill(c, carry):
            n1, n2 = carry
            sl = pl.ds(pl.multiple_of(c * tk, tk), tk)
            kc = k_ref[sl, :]
            kcat[sl, 0:LANES] = kc
            kcat[sl, LANES:2 * LANES] = ktab_ref[...]
            vcat[sl, 0:LANES] = v_ref[sl, :]
            vcat[sl, LANES:2 * LANES] = ones_col
            kf = kc.astype(F32)
            kk = kf * kf
            s1 = jnp.sum(jnp.where(lane_k < HD_A, kk, 0.0), axis=-1, keepdims=True)
            s2 = jnp.sum(jnp.where(lane_k >= HD_A, kk, 0.0), axis=-1, keepdims=True)
            return jnp.maximum(n1, s1), jnp.maximum(n2, s2)

        z = jnp.zeros((tk, 1), F32)
        n1, n2 = lax.fori_loop(0, n_chunks, fill, (z, z))
        kstat[0:1, :] = jnp.broadcast_to(jnp.sqrt(jnp.max(n1, axis=0, keepdims=True)), (1, LANES))
        kstat[1:2, :] = jnp.broadcast_to(jnp.sqrt(jnp.max(n2, axis=0, keepdims=True)), (1, LANES))

    q = q_ref[...]
    lane = lax.broadcasted_iota(jnp.int32, (tq, LANES), 1)
    qf = q.astype(F32)
    qq = qf * qf
    qn1 = jnp.sqrt(jnp.sum(jnp.where(lane < HD_A, qq, 0.0), axis=-1, keepdims=True))
    qn2 = jnp.sqrt(jnp.sum(jnp.where(lane >= HD_A, qq, 0.0), axis=-1, keepdims=True))
    bound = jnp.concatenate([qn1 * kstat[0:1, 0:1], qn2 * kstat[1:2, 0:1]], axis=0) * 1.01
    bound_ok = jnp.max(bound) <= BOUND_LIMIT

    zero = jnp.zeros_like(q)
    base = jnp.concatenate([jnp.where(lane < HD_A, q, zero), jnp.where(lane >= HD_A, q, zero)], axis=0)
    row2 = lax.broadcasted_iota(jnp.int32, (2 * tq, LANES), 0)
    lane2 = lax.broadcasted_iota(jnp.int32, (2 * tq, LANES), 1)
    iq = jnp.where(row2 >= tq, row2 - tq, row2).astype(F32)
    hi, mid, lo = _split3(cs * iq)
    ppos = jnp.where(lane2 == AUG_P, hi, jnp.where(lane2 == AUG_P + 1, mid,
                                                   jnp.where(lane2 == AUG_P + 2, lo, 0.0)))
    pslope = jnp.where((lane2 >= AUG_J) & (lane2 < AUG_J + 3), slope, 0.0)
    rcol = jnp.where(lane2 == AUG_R, REF_HEADROOM - bound, 0.0)
    q_left = jnp.concatenate([base, (rcol - ppos + pslope).astype(BF16)], axis=1)
    q_right = jnp.concatenate([base, (rcol + ppos - pslope).astype(BF16)], axis=1)
    q_diag = jnp.concatenate([base, rcol.astype(BF16)], axis=1)
    q_plain = jnp.concatenate([base, jnp.zeros_like(base)], axis=1)

    row = lax.broadcasted_iota(jnp.int32, (2 * tq, tq), 0)
    col = lax.broadcasted_iota(jnp.int32, (2 * tq, tq), 1)
    rel_own = (jnp.where(row >= tq, row - tq, row) - col).astype(F32)
    i0 = i * tq
    cd = i0 // tk

    def chunk(c):
        return pl.ds(pl.multiple_of(c * tk, tk), tk)

    def fixed_reference():
        acc[...] = jnp.zeros(acc.shape, F32)

        def folded(keys, left, base):
            off = cs * (i0 - base).astype(F32)
            s = lax.dot_general(jnp.where(left, q_left, q_right), kcat[keys, :], nt,
                                preferred_element_type=F32) + jnp.where(left, -off, off)
            acc[...] += jnp.dot(jnp.exp2(s).astype(BF16), vcat[keys, :], preferred_element_type=F32)

        def body(c, carry):
            cc = c + (c >= cd).astype(jnp.int32)
            folded(chunk(cc), cc < cd, cc * tk)
            return carry

        n_main = n_chunks - 1
        lax.fori_loop(0, n_main, body, 0, unroll=3 if n_main % 3 == 0 else 2 if n_main % 2 == 0 else 1)

        own = (i0 - cd * tk) // tq
        s = lax.dot_general(q_diag, kcat[pl.ds(pl.multiple_of(i0, tq), tq), :], nt, preferred_element_type=F32)
        s = s - cs * jnp.abs(rel_own)
        acc[...] += jnp.dot(jnp.exp2(s).astype(BF16), vcat[pl.ds(pl.multiple_of(i0, tq), tq), :],
                            preferred_element_type=F32)
        for k in range(1, tk // tq):
            blk = (own + k) % (tk // tq)
            keys = pl.ds(pl.multiple_of(cd * tk + blk * tq, tq), tq)
            folded(keys, blk < own, cd * tk)

    def online_reference():
        acc[...] = jnp.zeros(acc.shape, F32)
        m_scr[...] = jnp.full(m_scr.shape, NEG_INF, F32)

        def body(c, carry):
            keys = pl.ds(pl.multiple_of(c * tq, tq), tq)
            s = lax.dot_general(q_plain, kcat[keys, :], nt, preferred_element_type=F32)
            s = s - cs * jnp.abs(rel_own + (i0 - c * tq).astype(F32))
            m_prev = m_scr[...]
            m_new = jnp.maximum(m_prev, jnp.max(s, axis=-1, keepdims=True))
            p = jnp.exp2(s - m_new)
            acc[...] = jnp.exp2(m_prev - m_new) * acc[...] + jnp.dot(
                p.astype(BF16), vcat[keys, :], preferred_element_type=F32)
            m_scr[...] = m_new
            return carry

        lax.fori_loop(0, seq // tq, body, 0)

    lax.cond(bound_ok, fixed_reference, online_reference)

    lamv = lamv_ref[...]
    lam = (jnp.exp(jnp.sum(lamv[0:1] * lamv[1:2], axis=-1, keepdims=True))
           - jnp.exp(jnp.sum(lamv[2:3] * lamv[3:4], axis=-1, keepdims=True)) + lam_init)
    a = acc[...]
    o = a[:, 0:LANES] / a[:, LANES:LANES + 1]
    d = o[:tq] - lam * o[tq:]
    ms = jnp.mean(d * d, axis=-1, keepdims=True)
    o_ref[...] = (d * lax.rsqrt(ms + SUBLN_EPS) * gsub_ref[...] * (1.0 - lam_init)).astype(BF16)


def _diff_attn(z3, slopes_a, lamv, gsub, lam_init, tq, tk):
    B, S, _ = z3.shape
    assert tk % tq == 0 and S % tk == 0
    qb0, kb0, vb0 = COL_QA // LANES, COL_KA // LANES, COL_VA // LANES
    kernel = functools.partial(_diff_attn_kernel, tq=tq, tk=tk, seq=S, lam_init=lam_init)
    return pl.pallas_call(
        kernel,
        grid=(B, H_A, S // tq),
        in_specs=[
            pl.BlockSpec(memory_space=pltpu.SMEM),
            pl.BlockSpec((4, HD_A), lambda b, h, i: (0, 0)),
            pl.BlockSpec((None, tq, LANES), lambda b, h, i: (b, i, qb0 + h)),
            pl.BlockSpec((None, S, LANES), lambda b, h, i: (b, 0, kb0 + h)),
            pl.BlockSpec((None, S, LANES), lambda b, h, i: (b, 0, vb0 + h)),
            pl.BlockSpec((tk, LANES), lambda b, h, i: (0, 0)),
            pl.BlockSpec((1, LANES), lambda b, h, i: (0, 0)),
        ],
        out_specs=pl.BlockSpec((None, tq, LANES), lambda b, h, i: (b, i, h)),
        out_shape=jax.ShapeDtypeStruct((B, S, OA_W), BF16),
        scratch_shapes=[pltpu.VMEM((S, 2 * LANES), BF16), pltpu.VMEM((S, 2 * LANES), BF16),
                        pltpu.VMEM((8, LANES), F32), pltpu.VMEM((2 * tq, 2 * LANES), F32),
                        pltpu.VMEM((2 * tq, 1), F32)],
        compiler_params=pltpu.CompilerParams(
            dimension_semantics=("arbitrary", "arbitrary", "arbitrary"), vmem_limit_bytes=VMEM_LIMIT),
        name="diff_attn",
    )(slopes_a, lamv, z3, z3, z3, _key_table(tk), gsub)


def _band_attn_kernel(q_ref, kp_ref, kc_ref, kn_ref, vp_ref, vc_ref, vn_ref, o_ref, lse_ref,
                      *, length, dil, half, slopes):
    i = pl.program_id(1)
    qb = Q_BLOCK
    kw = 3 * qb
    row = lax.broadcasted_iota(jnp.int32, (qb, kw), 0)
    col = lax.broadcasted_iota(jnp.int32, (qb, kw), 1)
    rel = col - qb - row
    kpos = (i - 1) * qb + col
    valid = (jnp.abs(rel) <= half) & (kpos >= 0) & (kpos < length)
    dist = (dil * jnp.abs(rel)).astype(F32)
    nt = (((1,), (1,)), ((), ()))

    def one_class(c, carry):
        rows = pl.ds(c, qb, stride=dil) if dil > 1 else pl.ds(0, qb)
        for hh in range(H_B):
            sl = pl.ds(hh * HD_B, HD_B)
            kk = jnp.concatenate([kp_ref[c, :, sl], kc_ref[c, :, sl], kn_ref[c, :, sl]], axis=0)
            vv = jnp.concatenate([vp_ref[c, :, sl], vc_ref[c, :, sl], vn_ref[c, :, sl]], axis=0)
            s = lax.dot_general(q_ref[c, :, sl], kk, nt, preferred_element_type=F32)
            s = jnp.where(valid, s - float(slopes[hh]) * dist, NEG_INF)
            m = jnp.max(s, axis=-1, keepdims=True)
            p = jnp.exp(s - m)
            l = jnp.sum(p, axis=-1, keepdims=True)
            o_ref[hh, rows, :] = jnp.dot(p.astype(BF16), vv, preferred_element_type=F32) / l
            lse_ref[hh, rows, :] = jnp.broadcast_to(m + jnp.log(l), (qb, HD_B))
        return carry

    lax.fori_loop(0, dil, one_class, 0, unroll=2 if dil % 2 == 0 else 1)


def _band_attn(qkv, g, window, dil):
    _, B, _, L, _ = qkv.shape
    S = L * dil
    half = (window // 2) // dil
    nb = L // Q_BLOCK
    slopes = _alibi_slopes(N_GROUPS_B * H_B).reshape(N_GROUPS_B, H_B)[g]
    kernel = functools.partial(_band_attn_kernel, length=L, dil=dil, half=half, slopes=slopes)

    def spec(sec, shift):
        def imap(b, i):
            return (sec, b, 0, jnp.clip(i + shift, 0, nb - 1), 0)
        return pl.BlockSpec((None, None, dil, Q_BLOCK, TN), imap)

    out_spec = pl.BlockSpec((None, H_B, Q_BLOCK * dil, HD_B), lambda b, i: (b, 0, i, 0))
    out_shape = jax.ShapeDtypeStruct((B, H_B, S, HD_B), F32)
    return pl.pallas_call(
        kernel,
        grid=(B, nb),
        in_specs=[spec(0, 0), spec(1, -1), spec(1, 0), spec(1, 1), spec(2, -1), spec(2, 0), spec(2, 1)],
        out_specs=[out_spec, out_spec],
        out_shape=[out_shape, out_shape],
        compiler_params=pltpu.CompilerParams(
            dimension_semantics=("arbitrary", "arbitrary"), vmem_limit_bytes=VMEM_LIMIT),
        name=f"band_attn_g{g}",
    )(qkv, qkv, qkv, qkv, qkv, qkv, qkv)


def _merge_kernel(x_ref, ga_ref, gb_ref, oa_ref, o0_ref, o1_ref, o2_ref, l0_ref, l1_ref, l2_ref,
                  wpa_ref, wpb_ref, wo_ref, gffn_ref, x1_ref, h2_ref):
    heads = []
    for hh in range(H_B):
        l0, l1, l2 = l0_ref[hh], l1_ref[hh], l2_ref[hh]
        m = jnp.maximum(jnp.maximum(l0, l1), l2)
        w0, w1, w2 = jnp.exp(l0 - m), jnp.exp(l1 - m), jnp.exp(l2 - m)
        heads.append(((w0 * o0_ref[hh] + w1 * o1_ref[hh] + w2 * o2_ref[hh]) / (w0 + w1 + w2)).astype(BF16))
    ob = jnp.concatenate(heads, axis=1)
    pa = jnp.dot(oa_ref[...], wpa_ref[...], preferred_element_type=F32)
    pb = jnp.dot(ob, wpb_ref[...], preferred_element_type=F32)
    sigmoid = lambda g: 0.5 * jnp.tanh(0.5 * g.astype(F32)) + 0.5
    merged = sigmoid(ga_ref[...]) * pa + sigmoid(gb_ref[...]) * pb
    x1 = x_ref[...] + jnp.dot(merged.astype(BF16), wo_ref[...], preferred_element_type=F32)
    x1_ref[...] = x1
    ms = jnp.mean(x1 * x1, axis=-1, keepdims=True)
    h2_ref[...] = (x1 * lax.rsqrt(ms + NORM_EPS) * gffn_ref[...]).astype(BF16)


def _merge(x2, z2, oa2, obs, lses, wpa, wpb, wo, gffn, seq, tm):
    T = x2.shape[0]
    tiles_per_seq = seq // tm
    row = lambda w: pl.BlockSpec((tm, w), lambda i: (i, 0))
    per_head = pl.BlockSpec((None, H_B, tm, HD_B), lambda i: (i // tiles_per_seq, 0, i % tiles_per_seq, 0))
    const = lambda a: pl.BlockSpec(a.shape, lambda i: (0, 0), pipeline_mode=pl.Buffered(1))
    return pl.pallas_call(
        _merge_kernel,
        grid=(T // tm,),
        in_specs=[row(D_MODEL),
                  pl.BlockSpec((tm, D_MODEL), lambda i: (i, COL_GATE_A // D_MODEL)),
                  pl.BlockSpec((tm, D_MODEL), lambda i: (i, COL_GATE_B // D_MODEL)),
                  row(OA_W)] + [per_head] * 6 + [const(wpa), const(wpb), const(wo), const(gffn)],
        out_specs=[row(D_MODEL), row(D_MODEL)],
        out_shape=[jax.ShapeDtypeStruct((T, D_MODEL), F32), jax.ShapeDtypeStruct((T, D_MODEL), BF16)],
        compiler_params=pltpu.CompilerParams(
            dimension_semantics=("parallel",), vmem_limit_bytes=VMEM_LIMIT),
        name="merge",
    )(x2, z2, z2, oa2, *obs, *lses, wpa, wpb, wo, gffn)


HALO = 16


def _ffn_kernel(h_ref, hp_ref, hn_ref, x1_ref, wg_ref, wv_ref, cwg_ref, cwv_ref, cbg_ref, cbv_ref,
                wd_ref, o_ref, hcat_scr, acc_scr, *, tm, tiles_per_seq, n_f):
    i = pl.program_id(0)
    f = pl.program_id(1)

    @pl.when(f == 0)
    def _():
        pos = i % tiles_per_seq
        hp = hp_ref[...]
        hn = hn_ref[...]
        hcat_scr[0:HALO, :] = jnp.where(pos == 0, jnp.zeros_like(hp), hp)
        hcat_scr[HALO:HALO + tm, :] = h_ref[...]
        hcat_scr[HALO + tm:, :] = jnp.where(pos == tiles_per_seq - 1, jnp.zeros_like(hn), hn)
        acc_scr[...] = jnp.zeros(acc_scr.shape, F32)

    hcat = hcat_scr[...]
    rows = tm + 2 * HALO

    def conv_half(w_ref, cw_ref, cb_ref):
        a = jnp.dot(hcat, w_ref[...], preferred_element_type=F32)
        a_prev = pltpu.roll(a, 1, 0)[HALO:HALO + tm]
        a_next = pltpu.roll(a, rows - 1, 0)[HALO:HALO + tm]
        cw = cw_ref[...]
        return a_prev * cw[0:1] + a[HALO:HALO + tm] * cw[1:2] + a_next * cw[2:3] + cb_ref[...]

    ug = conv_half(wg_ref, cwg_ref, cbg_ref)
    uv = conv_half(wv_ref, cwv_ref, cbv_ref)
    act = 0.5 * ug * (1.0 + lax.erf(ug * np.float32(math.sqrt(0.5)))) * uv
    acc_scr[...] += jnp.dot(act.astype(BF16), wd_ref[...], preferred_element_type=F32)

    @pl.when(f == n_f - 1)
    def _():
        o_ref[...] = x1_ref[...] + acc_scr[...]


def _ffn(h2, x1, w_up, conv_w, conv_b, w_down, seq, tm, tf):
    T = h2.shape[0]
    n_f = D_FF // tf
    hb = tm // HALO
    n_hb = T // HALO
    kernel = functools.partial(_ffn_kernel, tm=tm, tiles_per_seq=seq // tm, n_f=n_f)
    return pl.pallas_call(
        kernel,
        grid=(T // tm, n_f),
        in_specs=[
            pl.BlockSpec((tm, D_MODEL), lambda i, f: (i, 0)),
            pl.BlockSpec((HALO, D_MODEL), lambda i, f: (jnp.maximum(i * hb - 1, 0), 0)),
            pl.BlockSpec((HALO, D_MODEL), lambda i, f: (jnp.minimum((i + 1) * hb, n_hb - 1), 0)),
            pl.BlockSpec((tm, D_MODEL), lambda i, f: (i, 0)),
            pl.BlockSpec((D_MODEL, tf), lambda i, f: (0, f)),
            pl.BlockSpec((D_MODEL, tf), lambda i, f: (0, f + n_f)),
            pl.BlockSpec((3, tf), lambda i, f: (0, f)),
            pl.BlockSpec((3, tf), lambda i, f: (0, f + n_f)),
            pl.BlockSpec((1, tf), lambda i, f: (0, f)),
            pl.BlockSpec((1, tf), lambda i, f: (0, f + n_f)),
            pl.BlockSpec((tf, D_MODEL), lambda i, f: (f, 0)),
        ],
        out_specs=pl.BlockSpec((tm, D_MODEL), lambda i, f: (i, 0)),
        out_shape=jax.ShapeDtypeStruct((T, D_MODEL), F32),
        scratch_shapes=[pltpu.VMEM((tm + 2 * HALO, D_MODEL), BF16), pltpu.VMEM((tm, D_MODEL), F32)],
        compiler_params=pltpu.CompilerParams(
            dimension_semantics=("parallel", "arbitrary"), vmem_limit_bytes=VMEM_LIMIT),
        name="ffn",
    )(h2, h2, h2, x1, w_up, w_up, conv_w, conv_w, conv_b, conv_b, w_down)


def _prep_params(g_mix_norm, w_in, g_qa, g_ka, lam_q1, lam_k1, lam_q2, lam_k2, g_subln, g_qb, g_kb,
                 w_pa, w_pb, w_o, g_ffn_norm, w_up, conv_w, conv_b, w_down):
    qa, ka, va, qb, kb, vb, gate_a, gate_b = jnp.split(
        w_in, list(np.cumsum([QA_W, QA_W, QA_W, QB_W, QB_W, QB_W, D_MODEL])), axis=-1)
    w_in_p = jnp.concatenate([gate_a, gate_b, qa, ka, va, qb, kb, vb], axis=-1).astype(BF16)
    ones = lambda n: jnp.ones((n,), F32)
    gvec = jnp.concatenate([
        ones(2 * D_MODEL),
        jnp.tile(g_qa.astype(F32), 2 * H_A) * (HD_A ** -0.5 * LOG2E),
        jnp.tile(g_ka.astype(F32), 2 * H_A),
        ones(QA_W),
        jnp.tile(g_qb.astype(F32), N_GROUPS_B * H_B) * (HD_B ** -0.5),
        jnp.tile(g_kb.astype(F32), N_GROUPS_B * H_B),
        ones(QB_W)]).reshape(1, N_IN)
    idx = np.arange(TN)
    gmat = jnp.asarray(np.stack([
        (idx[:, None] // HD_A == idx[None, :] // HD_A) / HD_A,
        (idx[:, None] // HD_B == idx[None, :] // HD_B) / HD_B]).astype(np.float32), BF16)
    lamv = jnp.stack([lam_q1, lam_k1, lam_q2, lam_k2]).astype(F32)
    return dict(
        gmix=g_mix_norm.astype(F32).reshape(1, D_MODEL), w_in=w_in_p, gvec=gvec, gmat=gmat, lamv=lamv,
        gsub=g_subln.astype(F32).reshape(1, 2 * HD_A), wpa=w_pa.astype(BF16), wpb=w_pb.astype(BF16),
        wo=w_o.astype(BF16), gffn=g_ffn_norm.astype(F32).reshape(1, D_MODEL), w_up=w_up.astype(BF16),
        conv_w=conv_w.astype(F32), conv_b=conv_b.astype(F32).reshape(1, 2 * D_FF),
        w_down=w_down.astype(BF16))


def _encoder_layer(x, layer_idx, p):
    B, S, _ = x.shape
    T = B * S
    lam_init = 0.8 - 0.6 * math.exp(-0.3 * layer_idx)
    x2 = x.reshape(T, D_MODEL)
    z2, *groups = _in_proj(x2, p["gmix"], p["w_in"], p["gvec"], p["gmat"], batch=B, seq=S, tm=min(1024, S))
    z3 = z2.reshape(B, S, COL_QB)
    oa = _diff_attn(z3, jnp.asarray(_alibi_slopes(H_A)), p["lamv"], p["gsub"], lam_init,
                    tq=min(512, S // 2), tk=min(1024, S // 2))
    obs, lses = [], []
    for g, (window, dil) in enumerate(DIL_CONFIGS):
        o, lse = _band_attn(groups[g], g, window, dil)
        obs.append(o)
        lses.append(lse)
    x1, h2 = _merge(x2, z2, oa.reshape(T, OA_W), obs, lses, p["wpa"], p["wpb"], p["wo"], p["gffn"],
                    seq=S, tm=256)
    y = _ffn(h2, x1, p["w_up"], p["conv_w"], p["conv_b"], p["w_down"], seq=S, tm=512, tf=512)
    return y.reshape(B, S, D_MODEL)


def kernel(x_prompt, x_sample, g_mix_norm, w_in, g_qa, g_ka, lam_q1, lam_k1, lam_q2, lam_k2, g_subln,
           g_qb, g_kb, w_pa, w_pb, w_o, g_ffn_norm, w_up, conv_w, conv_b, w_down):
    y_prompt, y_sample = x_prompt, x_sample
    for l in range(w_in.shape[0]):
        p = _prep_params(g_mix_norm[l], w_in[l], g_qa[l], g_ka[l], lam_q1[l], lam_k1[l], lam_q2[l],
                         lam_k2[l], g_subln[l], g_qb[l], g_kb[l], w_pa[l], w_pb[l], w_o[l],
                         g_ffn_norm[l], w_up[l], conv_w[l], conv_b[l], w_down[l])
        y_prompt = _encoder_layer(y_prompt, l, p)
        y_sample = _encoder_layer(y_sample, l, p)
    return (y_prompt, y_sample)
```

```python
import functools
import math

import numpy as np
import jax
import jax.numpy as jnp
from jax import lax
from jax.experimental import pallas as pl
from jax.experimental.pallas import tpu as pltpu

D_MODEL = 2048
H_A = 8
HD_A = 64
DIL_CONFIGS = ((128, 1), (512, 4), (2048, 16))
N_GROUPS_B = 3
H_B = 4
HD_B = 128
D_FF = 5632
Q_BLOCK = 128
NORM_EPS = 1e-6
SUBLN_EPS = 1e-5
NEG_INF = -1e30

QA_W = H_A * 2 * HD_A
QB_W = N_GROUPS_B * H_B * HD_B
OA_W = QA_W
OB_W = H_B * HD_B
N_IN = 3 * QA_W + 3 * QB_W + 2 * D_MODEL

LANES = 128
COL_GATE_A = 0
COL_GATE_B = D_MODEL
COL_QA = 2 * D_MODEL
COL_KA = COL_QA + QA_W
COL_VA = COL_KA + QA_W
COL_QB = COL_VA + QA_W
COL_KB = COL_QB + QB_W
COL_VB = COL_KB + QB_W
TN = 512
N_TILES = N_IN // TN
VMEM_LIMIT = 56 * 1024 * 1024

F32 = jnp.float32
BF16 = jnp.bfloat16


def _alibi_slopes(n):
    return np.asarray(2.0 ** (-8.0 * (np.arange(n) + 1) / n), dtype=np.float32)


T_QA, T_VA, T_QB, T_KB, T_VB = COL_QA // TN, COL_VA // TN, COL_QB // TN, COL_KB // TN, COL_VB // TN
DILS = tuple(d for _, d in DIL_CONFIGS)


def _in_proj_kernel(x_ref, gmix_ref, w_ref, gvec_ref, gmat_ref, z_ref, g0_ref, g1_ref, g2_ref,
                    h_scr, t_scr, *, tm):
    j = pl.program_id(1)

    @pl.when(j == 0)
    def _():
        x = x_ref[...]
        ms = jnp.mean(x * x, axis=-1, keepdims=True)
        h_scr[...] = (x * lax.rsqrt(ms + NORM_EPS) * gmix_ref[...]).astype(BF16)

    def plain():
        return jnp.dot(h_scr[...], w_ref[...], preferred_element_type=F32)

    def normed():
        z = plain()
        ms = jnp.dot((z * z).astype(BF16), gmat_ref[...], preferred_element_type=F32)
        return z * lax.rsqrt(ms + NORM_EPS) * gvec_ref[...]

    @pl.when((j >= T_QA) & (j < T_VA))
    def _():
        z_ref[...] = normed().astype(BF16)

    @pl.when((j < T_QA) | ((j >= T_VA) & (j < T_QB)))
    def _():
        z_ref[...] = plain().astype(BF16)

    def scatter(g_ref, dil, val):
        if dil == 1:
            g_ref[0] = val.astype(BF16)
        else:
            for kb in range(TN // LANES):
                t_scr[kb] = val[:, kb * LANES:(kb + 1) * LANES]
            for c in range(dil):
                for kb in range(TN // LANES):
                    g_ref[c, :, kb * LANES:(kb + 1) * LANES] = (
                        t_scr[kb, pl.ds(c, tm // dil, stride=dil), :].astype(BF16))

    for g, (g_ref, dil) in enumerate(zip((g0_ref, g1_ref, g2_ref), DILS)):
        @pl.when((j == T_QB + g) | (j == T_KB + g))
        def _(g_ref=g_ref, dil=dil):
            scatter(g_ref, dil, normed())

        @pl.when(j == T_VB + g)
        def _(g_ref=g_ref, dil=dil):
            scatter(g_ref, dil, plain())


def _in_proj(x2, gmix, w_in_p, gvec, gmat, batch, seq, tm):
    T = x2.shape[0]
    tiles_per_seq = seq // tm
    main_w = COL_QB

    def group_spec(g, dil):
        def imap(i, j):
            sec = (j >= T_KB + g).astype(jnp.int32) + (j >= T_VB + g).astype(jnp.int32)
            return (sec, i // tiles_per_seq, 0, i % tiles_per_seq, 0)
        return pl.BlockSpec((None, None, dil, tm // dil, TN), imap)

    return pl.pallas_call(
        functools.partial(_in_proj_kernel, tm=tm),
        grid=(T // tm, N_TILES),
        in_specs=[
            pl.BlockSpec((tm, D_MODEL), lambda i, j: (i, 0)),
            pl.BlockSpec((1, D_MODEL), lambda i, j: (0, 0)),
            pl.BlockSpec((D_MODEL, TN), lambda i, j: (0, j)),
            pl.BlockSpec((1, TN), lambda i, j: (0, j)),
            pl.BlockSpec((None, TN, TN), lambda i, j: (jnp.where(j < T_QB, 0, 1), 0, 0)),
        ],
        out_specs=[pl.BlockSpec((tm, TN), lambda i, j: (i, jnp.minimum(j, T_QB - 1)))]
                  + [group_spec(g, dil) for g, dil in enumerate(DILS)],
        out_shape=[jax.ShapeDtypeStruct((T, main_w), BF16)]
                  + [jax.ShapeDtypeStruct((3, batch, dil, seq // dil, TN), BF16) for dil in DILS],
        scratch_shapes=[pltpu.VMEM((tm, D_MODEL), BF16), pltpu.VMEM((TN // LANES, tm, LANES), F32)],
        compiler_params=pltpu.CompilerParams(
            dimension_semantics=("arbitrary", "arbitrary"), vmem_limit_bytes=VMEM_LIMIT),
        name="in_proj",
    )(x2, gmix, w_in_p, gvec, gmat)


LOG2E = math.log2(math.e)
AUG_P, AUG_J, AUG_R = 0, 3, 6
REF_HEADROOM = 60.0
BOUND_LIMIT = 80.0


def _split3(x):
    def top(a):
        bits = lax.bitcast_convert_type(a, jnp.uint32) & jnp.uint32(0xFFFF0000)
        return lax.bitcast_convert_type(bits, F32)
    hi = top(x)
    mid = top(x - hi)
    lo = top(x - hi - mid)
    return hi, mid, lo


def _key_table(tk):
    def top(a):
        return (a.view(np.uint32) & np.uint32(0xFFFF0000)).view(np.float32)
    x = (np.arange(tk, dtype=np.float32) * np.float32(LOG2E)).astype(np.float32)
    hi = top(x)
    mid = top(x - hi)
    lo = top(x - hi - mid)
    tab = np.zeros((tk, LANES), np.float32)
    tab[:, AUG_P:AUG_P + 3] = 1.0
    tab[:, AUG_J], tab[:, AUG_J + 1], tab[:, AUG_J + 2] = hi, mid, lo
    tab[:, AUG_R] = 1.0
    return jnp.asarray(tab).astype(BF16)


def _diff_attn_kernel(slopes_ref, lamv_ref, q_ref, k_ref, v_ref, ktab_ref, gsub_ref, o_ref,
                      kcat, vcat, kstat, acc, m_scr, *, tq, tk, seq, lam_init):
    h = pl.program_id(1)
    i = pl.program_id(2)
    n_chunks = seq // tk
    slope = slopes_ref[h]
    cs = slope * LOG2E
    nt = (((1,), (1,)), ((), ()))

    @pl.when(i == 0)
    def _():
        lane_k = lax.broadcasted_iota(jnp.int32, (tk, LANES), 1)
        ones_col = jnp.where(lane_k == 0, 1.0, 0.0).astype(BF16)

        def fill(c, carry):
            n1, n2 = carry
            sl = pl.ds(pl.multiple_of(c * tk, tk), tk)
            kc = k_ref[sl, :]
            kcat[sl, 0:LANES] = kc
            kcat[sl, LANES:2 * LANES] = ktab_ref[...]
            vcat[sl, 0:LANES] = v_ref[sl, :]
            vcat[sl, LANES:2 * LANES] = ones_col
            kf = kc.astype(F32)
            kk = kf * kf
            s1 = jnp.sum(jnp.where(lane_k < HD_A, kk, 0.0), axis=-1, keepdims=True)
            s2 = jnp.sum(jnp.where(lane_k >= HD_A, kk, 0.0), axis=-1, keepdims=True)
            return jnp.maximum(n1, s1), jnp.maximum(n2, s2)

        z = jnp.zeros((tk, 1), F32)
        n1, n2 = lax.fori_loop(0, n_chunks, fill, (z, z))
        kstat[0:1, :] = jnp.broadcast_to(jnp.sqrt(jnp.max(n1, axis=0, keepdims=True)), (1, LANES))
        kstat[1:2, :] = jnp.broadcast_to(jnp.sqrt(jnp.max(n2, axis=0, keepdims=True)), (1, LANES))

    q = q_ref[...]
    lane = lax.broadcasted_iota(jnp.int32, (tq, LANES), 1)
    qf = q.astype(F32)
    qq = qf * qf
    qn1 = jnp.sqrt(jnp.sum(jnp.where(lane < HD_A, qq, 0.0), axis=-1, keepdims=True))
    qn2 = jnp.sqrt(jnp.sum(jnp.where(lane >= HD_A, qq, 0.0), axis=-1, keepdims=True))
    bound = jnp.concatenate([qn1 * kstat[0:1, 0:1], qn2 * kstat[1:2, 0:1]], axis=0) * 1.01
    bound_ok = jnp.max(bound) <= BOUND_LIMIT

    zero = jnp.zeros_like(q)
    base = jnp.concatenate([jnp.where(lane < HD_A, q, zero), jnp.where(lane >= HD_A, q, zero)], axis=0)
    row2 = lax.broadcasted_iota(jnp.int32, (2 * tq, LANES), 0)
    lane2 = lax.broadcasted_iota(jnp.int32, (2 * tq, LANES), 1)
    iq = jnp.where(row2 >= tq, row2 - tq, row2).astype(F32)
    hi, mid, lo = _split3(cs * iq)
    ppos = jnp.where(lane2 == AUG_P, hi, jnp.where(lane2 == AUG_P + 1, mid,
                                                   jnp.where(lane2 == AUG_P + 2, lo, 0.0)))
    pslope = jnp.where((lane2 >= AUG_J) & (lane2 < AUG_J + 3), slope, 0.0)
    rcol = jnp.where(lane2 == AUG_R, REF_HEADROOM - bound, 0.0)
    q_left = jnp.concatenate([base, (rcol - ppos + pslope).astype(BF16)], axis=1)
    q_right = jnp.concatenate([base, (rcol + ppos - pslope).astype(BF16)], axis=1)
    q_diag = jnp.concatenate([base, rcol.astype(BF16)], axis=1)
    q_plain = jnp.concatenate([base, jnp.zeros_like(base)], axis=1)

    row = lax.broadcasted_iota(jnp.int32, (2 * tq, tq), 0)
    col = lax.broadcasted_iota(jnp.int32, (2 * tq, tq), 1)
    rel_own = (jnp.where(row >= tq, row - tq, row) - col).astype(F32)
    i0 = i * tq
    cd = i0 // tk

    def chunk(c):
        return pl.ds(pl.multiple_of(c * tk, tk), tk)

    def fixed_reference():
        acc[...] = jnp.zeros(acc.shape, F32)

        def folded(keys, left, base):
            off = cs * (i0 - base).astype(F32)
            s = lax.dot_general(jnp.where(left, q_left, q_right), kcat[keys, :], nt,
                                preferred_element_type=F32) + jnp.where(left, -off, off)
            acc[...] += jnp.dot(jnp.exp2(s).astype(BF16), vcat[keys, :], preferred_element_type=F32)

        def body(c, carry):
            cc = c + (c >= cd).astype(jnp.int32)
            folded(chunk(cc), cc < cd, cc * tk)
            return carry

        n_main = n_chunks - 1
        lax.fori_loop(0, n_main, body, 0, unroll=3 if n_main % 3 == 0 else 2 if n_main % 2 == 0 else 1)

        own = (i0 - cd * tk) // tq
        s = lax.dot_general(q_diag, kcat[pl.ds(pl.multiple_of(i0, tq), tq), :], nt, preferred_element_type=F32)
        s = s - cs * jnp.abs(rel_own)
        acc[...] += jnp.dot(jnp.exp2(s).astype(BF16), vcat[pl.ds(pl.multiple_of(i0, tq), tq), :],
                            preferred_element_type=F32)
        for k in range(1, tk // tq):
            blk = (own + k) % (tk // tq)
            keys = pl.ds(pl.multiple_of(cd * tk + blk * tq, tq), tq)
            folded(keys, blk < own, cd * tk)

    def online_reference():
        acc[...] = jnp.zeros(acc.shape, F32)
        m_scr[...] = jnp.full(m_scr.shape, NEG_INF, F32)

        def body(c, carry):
            keys = pl.ds(pl.multiple_of(c * tq, tq), tq)
            s = lax.dot_general(q_plain, kcat[keys, :], nt, preferred_element_type=F32)
            s = s - cs * jnp.abs(rel_own + (i0 - c * tq).astype(F32))
            m_prev = m_scr[...]
            m_new = jnp.maximum(m_prev, jnp.max(s, axis=-1, keepdims=True))
            p = jnp.exp2(s - m_new)
            acc[...] = jnp.exp2(m_prev - m_new) * acc[...] + jnp.dot(
                p.astype(BF16), vcat[keys, :], preferred_element_type=F32)
            m_scr[...] = m_new
            return carry

        lax.fori_loop(0, seq // tq, body, 0)

    lax.cond(bound_ok, fixed_reference, online_reference)

    lamv = lamv_ref[...]
    lam = (jnp.exp(jnp.sum(lamv[0:1] * lamv[1:2], axis=-1, keepdims=True))
           - jnp.exp(jnp.sum(lamv[2:3] * lamv[3:4], axis=-1, keepdims=True)) + lam_init)
    a = acc[...]
    o = a[:, 0:LANES] / a[:, LANES:LANES + 1]
    d = o[:tq] - lam * o[tq:]
    ms = jnp.mean(d * d, axis=-1, keepdims=True)
    o_ref[...] = (d * lax.rsqrt(ms + SUBLN_EPS) * gsub_ref[...] * (1.0 - lam_init)).astype(BF16)


def _diff_attn(z3, slopes_a, lamv, gsub, lam_init, tq, tk):
    B, S, _ = z3.shape
    assert tk % tq == 0 and S % tk == 0
    qb0, kb0, vb0 = COL_QA // LANES, COL_KA // LANES, COL_VA // LANES
    kernel = functools.partial(_diff_attn_kernel, tq=tq, tk=tk, seq=S, lam_init=lam_init)
    return pl.pallas_call(
        kernel,
        grid=(B, H_A, S // tq),
        in_specs=[
            pl.BlockSpec(memory_space=pltpu.SMEM),
            pl.BlockSpec((4, HD_A), lambda b, h, i: (0, 0)),
            pl.BlockSpec((None, tq, LANES), lambda b, h, i: (b, i, qb0 + h)),
            pl.BlockSpec((None, S, LANES), lambda b, h, i: (b, 0, kb0 + h)),
            pl.BlockSpec((None, S, LANES), lambda b, h, i: (b, 0, vb0 + h)),
            pl.BlockSpec((tk, LANES), lambda b, h, i: (0, 0)),
            pl.BlockSpec((1, LANES), lambda b, h, i: (0, 0)),
        ],
        out_specs=pl.BlockSpec((None, tq, LANES), lambda b, h, i: (b, i, h)),
        out_shape=jax.ShapeDtypeStruct((B, S, OA_W), BF16),
        scratch_shapes=[pltpu.VMEM((S, 2 * LANES), BF16), pltpu.VMEM((S, 2 * LANES), BF16),
                        pltpu.VMEM((8, LANES), F32), pltpu.VMEM((2 * tq, 2 * LANES), F32),
                        pltpu.VMEM((2 * tq, 1), F32)],
        compiler_params=pltpu.CompilerParams(
            dimension_semantics=("arbitrary", "arbitrary", "arbitrary"), vmem_limit_bytes=VMEM_LIMIT),
        name="diff_attn",
    )(slopes_a, lamv, z3, z3, z3, _key_table(tk), gsub)


def _band_attn_kernel(q_ref, kp_ref, kc_ref, kn_ref, vp_ref, vc_ref, vn_ref, o_ref, lse_ref,
                      *, length, dil, half, slopes):
    i = pl.program_id(1)
    qb = Q_BLOCK
    kw = qb + 2 * half
    nrow = H_B * qb
    row = lax.broadcasted_iota(jnp.int32, (nrow, kw), 0)
    col = lax.broadcasted_iota(jnp.int32, (nrow, kw), 1)
    rel = col - half - (row & (qb - 1))
    kpos = i * qb - half + col
    valid = (jnp.abs(rel) <= half) & (kpos >= 0) & (kpos < length)
    slope_rows = jnp.full((nrow, kw), float(slopes[0]), F32)
    for hh in range(1, H_B):
        slope_rows = jnp.where(row >= hh * qb, float(slopes[hh]), slope_rows)
    bias = slope_rows * (dil * jnp.abs(rel)).astype(F32)
    lane_head = lax.broadcasted_iota(jnp.int32, (qb, TN), 1) // HD_B
    nt = (((1,), (1,)), ((), ()))

    def one_class(c, carry):
        rows = pl.ds(c, qb, stride=dil) if dil > 1 else pl.ds(0, qb)
        q4 = q_ref[c]
        zero = jnp.zeros_like(q4)
        qbd = jnp.concatenate([jnp.where(lane_head == hh, q4, zero) for hh in range(H_B)], axis=0)
        kk = jnp.concatenate([kp_ref[c, qb - half:qb, :], kc_ref[c], kn_ref[c, 0:half, :]], axis=0)
        vv = jnp.concatenate([vp_ref[c, qb - half:qb, :], vc_ref[c], vn_ref[c, 0:half, :]], axis=0)
        s = lax.dot_general(qbd, kk, nt, preferred_element_type=F32)
        s = jnp.where(valid, s - bias, NEG_INF)
        m = jnp.max(s, axis=-1, keepdims=True)
        p = jnp.exp(s - m)
        l = jnp.sum(p, axis=-1, keepdims=True)
        o_all = jnp.dot(p.astype(BF16), vv, preferred_element_type=F32)
        lse = m + jnp.log(l)
        for hh in range(H_B):
            rr = slice(hh * qb, (hh + 1) * qb)
            o_ref[hh, rows, :] = o_all[rr, hh * HD_B:(hh + 1) * HD_B] / l[rr]
            lse_ref[hh, rows, :] = jnp.broadcast_to(lse[rr], (qb, HD_B))
        return carry

    lax.fori_loop(0, dil, one_class, 0, unroll=2 if dil % 2 == 0 else 1)


def _band_attn(qkv, g, window, dil):
    _, B, _, L, _ = qkv.shape
    S = L * dil
    half = (window // 2) // dil
    nb = L // Q_BLOCK
    slopes = _alibi_slopes(N_GROUPS_B * H_B).reshape(N_GROUPS_B, H_B)[g]
    kernel = functools.partial(_band_attn_kernel, length=L, dil=dil, half=half, slopes=slopes)

    def spec(sec, shift):
        def imap(b, i):
            return (sec, b, 0, jnp.clip(i + shift, 0, nb - 1), 0)
        return pl.BlockSpec((None, None, dil, Q_BLOCK, TN), imap)

    out_spec = pl.BlockSpec((None, H_B, Q_BLOCK * dil, HD_B), lambda b, i: (b, 0, i, 0))
    out_shape = jax.ShapeDtypeStruct((B, H_B, S, HD_B), F32)
    return pl.pallas_call(
        kernel,
        grid=(B, nb),
        in_specs=[spec(0, 0), spec(1, -1), spec(1, 0), spec(1, 1), spec(2, -1), spec(2, 0), spec(2, 1)],
        out_specs=[out_spec, out_spec],
        out_shape=[out_shape, out_shape],
        compiler_params=pltpu.CompilerParams(
            dimension_semantics=("arbitrary", "arbitrary"), vmem_limit_bytes=VMEM_LIMIT),
        name=f"band_attn_g{g}",
    )(qkv, qkv, qkv, qkv, qkv, qkv, qkv)


def _merge_kernel(x_ref, ga_ref, gb_ref, oa_ref, o0_ref, o1_ref, o2_ref, l0_ref, l1_ref, l2_ref,
                  wpa_ref, wpb_ref, wo_ref, gffn_ref, x1_ref, h2_ref):
    heads = []
    for hh in range(H_B):
        l0, l1, l2 = l0_ref[hh], l1_ref[hh], l2_ref[hh]
        m = jnp.maximum(jnp.maximum(l0, l1), l2)
        w0, w1, w2 = jnp.exp(l0 - m), jnp.exp(l1 - m), jnp.exp(l2 - m)
        heads.append(((w0 * o0_ref[hh] + w1 * o1_ref[hh] + w2 * o2_ref[hh]) / (w0 + w1 + w2)).astype(BF16))
    ob = jnp.concatenate(heads, axis=1)
    pa = jnp.dot(oa_ref[...], wpa_ref[...], preferred_element_type=F32)
    pb = jnp.dot(ob, wpb_ref[...], preferred_element_type=F32)
    sigmoid = lambda g: 0.5 * jnp.tanh(0.5 * g.astype(F32)) + 0.5
    merged = sigmoid(ga_ref[...]) * pa + sigmoid(gb_ref[...]) * pb
    x1 = x_ref[...] + jnp.dot(merged.astype(BF16), wo_ref[...], preferred_element_type=F32)
    x1_ref[...] = x1
    ms = jnp.mean(x1 * x1, axis=-1, keepdims=True)
    h2_ref[...] = (x1 * lax.rsqrt(ms + NORM_EPS) * gffn_ref[...]).astype(BF16)


def _merge(x2, z2, oa2, obs, lses, wpa, wpb, wo, gffn, seq, tm):
    T = x2.shape[0]
    tiles_per_seq = seq // tm
    row = lambda w: pl.BlockSpec((tm, w), lambda i: (i, 0))
    per_head = pl.BlockSpec((None, H_B, tm, HD_B), lambda i: (i // tiles_per_seq, 0, i % tiles_per_seq, 0))
    const = lambda a: pl.BlockSpec(a.shape, lambda i: (0, 0), pipeline_mode=pl.Buffered(1))
    return pl.pallas_call(
        _merge_kernel,
        grid=(T // tm,),
        in_specs=[row(D_MODEL),
                  pl.BlockSpec((tm, D_MODEL), lambda i: (i, COL_GATE_A // D_MODEL)),
                  pl.BlockSpec((tm, D_MODEL), lambda i: (i, COL_GATE_B // D_MODEL)),
                  row(OA_W)] + [per_head] * 6 + [const(wpa), const(wpb), const(wo), const(gffn)],
        out_specs=[row(D_MODEL), row(D_MODEL)],
        out_shape=[jax.ShapeDtypeStruct((T, D_MODEL), F32), jax.ShapeDtypeStruct((T, D_MODEL), BF16)],
        compiler_params=pltpu.CompilerParams(
            dimension_semantics=("parallel",), vmem_limit_bytes=VMEM_LIMIT),
        name="merge",
    )(x2, z2, z2, oa2, *obs, *lses, wpa, wpb, wo, gffn)


HALO = 16


def _ffn_kernel(h_ref, hp_ref, hn_ref, x1_ref, wg_ref, wv_ref, cwg_ref, cwv_ref, cbg_ref, cbv_ref,
                wd_ref, o_ref, hcat_scr, acc_scr, *, tm, tiles_per_seq, n_f):
    i = pl.program_id(0)
    f = pl.program_id(1)

    @pl.when(f == 0)
    def _():
        pos = i % tiles_per_seq
        hp = hp_ref[...]
        hn = hn_ref[...]
        hcat_scr[0:HALO, :] = jnp.where(pos == 0, jnp.zeros_like(hp), hp)
        hcat_scr[HALO:HALO + tm, :] = h_ref[...]
        hcat_scr[HALO + tm:, :] = jnp.where(pos == tiles_per_seq - 1, jnp.zeros_like(hn), hn)
        acc_scr[...] = jnp.zeros(acc_scr.shape, F32)

    hcat = hcat_scr[...]
    rows = tm + 2 * HALO

    def conv_half(w_ref, cw_ref, cb_ref):
        a = jnp.dot(hcat, w_ref[...], preferred_element_type=F32)
        a_prev = pltpu.roll(a, 1, 0)[HALO:HALO + tm]
        a_next = pltpu.roll(a, rows - 1, 0)[HALO:HALO + tm]
        cw = cw_ref[...]
        return a_prev * cw[0:1] + a[HALO:HALO + tm] * cw[1:2] + a_next * cw[2:3] + cb_ref[...]

    ug = conv_half(wg_ref, cwg_ref, cbg_ref)
    uv = conv_half(wv_ref, cwv_ref, cbv_ref)
    act = 0.5 * ug * (1.0 + lax.erf(ug * np.float32(math.sqrt(0.5)))) * uv
    acc_scr[...] += jnp.dot(act.astype(BF16), wd_ref[...], preferred_element_type=F32)

    @pl.when(f == n_f - 1)
    def _():
        o_ref[...] = x1_ref[...] + acc_scr[...]


def _ffn(h2, x1, w_up, conv_w, conv_b, w_down, seq, tm, tf):
    T = h2.shape[0]
    n_f = D_FF // tf
    hb = tm // HALO
    n_hb = T // HALO
    kernel = functools.partial(_ffn_kernel, tm=tm, tiles_per_seq=seq // tm, n_f=n_f)
    return pl.pallas_call(
        kernel,
        grid=(T // tm, n_f),
        in_specs=[
            pl.BlockSpec((tm, D_MODEL), lambda i, f: (i, 0)),
            pl.BlockSpec((HALO, D_MODEL), lambda i, f: (jnp.maximum(i * hb - 1, 0), 0)),
            pl.BlockSpec((HALO, D_MODEL), lambda i, f: (jnp.minimum((i + 1) * hb, n_hb - 1), 0)),
            pl.BlockSpec((tm, D_MODEL), lambda i, f: (i, 0)),
            pl.BlockSpec((D_MODEL, tf), lambda i, f: (0, f)),
            pl.BlockSpec((D_MODEL, tf), lambda i, f: (0, f + n_f)),
            pl.BlockSpec((3, tf), lambda i, f: (0, f)),
            pl.BlockSpec((3, tf), lambda i, f: (0, f + n_f)),
            pl.BlockSpec((1, tf), lambda i, f: (0, f)),
            pl.BlockSpec((1, tf), lambda i, f: (0, f + n_f)),
            pl.BlockSpec((tf, D_MODEL), lambda i, f: (f, 0)),
        ],
        out_specs=pl.BlockSpec((tm, D_MODEL), lambda i, f: (i, 0)),
        out_shape=jax.ShapeDtypeStruct((T, D_MODEL), F32),
        scratch_shapes=[pltpu.VMEM((tm + 2 * HALO, D_MODEL), BF16), pltpu.VMEM((tm, D_MODEL), F32)],
        compiler_params=pltpu.CompilerParams(
            dimension_semantics=("parallel", "arbitrary"), vmem_limit_bytes=VMEM_LIMIT),
        name="ffn",
    )(h2, h2, h2, x1, w_up, w_up, conv_w, conv_w, conv_b, conv_b, w_down)


def _prep_params(g_mix_norm, w_in, g_qa, g_ka, lam_q1, lam_k1, lam_q2, lam_k2, g_subln, g_qb, g_kb,
                 w_pa, w_pb, w_o, g_ffn_norm, w_up, conv_w, conv_b, w_down):
    qa, ka, va, qb, kb, vb, gate_a, gate_b = jnp.split(
        w_in, list(np.cumsum([QA_W, QA_W, QA_W, QB_W, QB_W, QB_W, D_MODEL])), axis=-1)
    w_in_p = jnp.concatenate([gate_a, gate_b, qa, ka, va, qb, kb, vb], axis=-1).astype(BF16)
    ones = lambda n: jnp.ones((n,), F32)
    gvec = jnp.concatenate([
        ones(2 * D_MODEL),
        jnp.tile(g_qa.astype(F32), 2 * H_A) * (HD_A ** -0.5 * LOG2E),
        jnp.tile(g_ka.astype(F32), 2 * H_A),
        ones(QA_W),
        jnp.tile(g_qb.astype(F32), N_GROUPS_B * H_B) * (HD_B ** -0.5),
        jnp.tile(g_kb.astype(F32), N_GROUPS_B * H_B),
        ones(QB_W)]).reshape(1, N_IN)
    idx = np.arange(TN)
    gmat = jnp.asarray(np.stack([
        (idx[:, None] // HD_A == idx[None, :] // HD_A) / HD_A,
        (idx[:, None] // HD_B == idx[None, :] // HD_B) / HD_B]).astype(np.float32), BF16)
    lamv = jnp.stack([lam_q1, lam_k1, lam_q2, lam_k2]).astype(F32)
    return dict(
        gmix=g_mix_norm.astype(F32).reshape(1, D_MODEL), w_in=w_in_p, gvec=gvec, gmat=gmat, lamv=lamv,
        gsub=g_subln.astype(F32).reshape(1, 2 * HD_A), wpa=w_pa.astype(BF16), wpb=w_pb.astype(BF16),
        wo=w_o.astype(BF16), gffn=g_ffn_norm.astype(F32).reshape(1, D_MODEL), w_up=w_up.astype(BF16),
        conv_w=conv_w.astype(F32), conv_b=conv_b.astype(F32).reshape(1, 2 * D_FF),
        w_down=w_down.astype(BF16))


def _encoder_layer(x, layer_idx, p):
    B, S, _ = x.shape
    T = B * S
    lam_init = 0.8 - 0.6 * math.exp(-0.3 * layer_idx)
    x2 = x.reshape(T, D_MODEL)
    z2, *groups = _in_proj(x2, p["gmix"], p["w_in"], p["gvec"], p["gmat"], batch=B, seq=S, tm=min(1024, S))
    z3 = z2.reshape(B, S, COL_QB)
    oa = _diff_attn(z3, jnp.asarray(_alibi_slopes(H_A)), p["lamv"], p["gsub"], lam_init,
                    tq=min(512, S // 2), tk=min(1024, S // 2))
    obs, lses = [], []
    for g, (window, dil) in enumerate(DIL_CONFIGS):
        o, lse = _band_attn(groups[g], g, window, dil)
        obs.append(o)
        lses.append(lse)
    x1, h2 = _merge(x2, z2, oa.reshape(T, OA_W), obs, lses, p["wpa"], p["wpb"], p["wo"], p["gffn"],
                    seq=S, tm=256)
    y = _ffn(h2, x1, p["w_up"], p["conv_w"], p["conv_b"], p["w_down"], seq=S, tm=512, tf=512)
    return y.reshape(B, S, D_MODEL)


def kernel(x_prompt, x_sample, g_mix_norm, w_in, g_qa, g_ka, lam_q1, lam_k1, lam_q2, lam_k2, g_subln,
           g_qb, g_kb, w_pa, w_pb, w_o, g_ffn_norm, w_up, conv_w, conv_b, w_down):
    y_prompt, y_sample = x_prompt, x_sample
    for l in range(w_in.shape[0]):
        p = _prep_params(g_mix_norm[l], w_in[l], g_qa[l], g_ka[l], lam_q1[l], lam_k1[l], lam_q2[l],
                         lam_k2[l], g_subln[l], g_qb[l], g_kb[l], w_pa[l], w_pb[l], w_o[l],
                         g_ffn_norm[l], w_up[l], conv_w[l], conv_b[l], w_down[l])
        y_prompt = _encoder_layer(y_prompt, l, p)
        y_sample = _encoder_layer(y_sample, l, p)
    return (y_prompt, y_sample)
```

```python
import functools
import math

import numpy as np
import jax
import jax.numpy as jnp
from jax import lax
from jax.experimental import pallas as pl
from jax.experimental.pallas import tpu as pltpu

D_MODEL = 2048
H_A = 8
HD_A = 64
DIL_CONFIGS = ((128, 1), (512, 4), (2048, 16))
N_GROUPS_B = 3
H_B = 4
HD_B = 128
D_FF = 5632
Q_BLOCK = 128
NORM_EPS = 1e-6
SUBLN_EPS = 1e-5
NEG_INF = -1e30

QA_W = H_A * 2 * HD_A
QB_W = N_GROUPS_B * H_B * HD_B
OA_W = QA_W
OB_W = H_B * HD_B
N_IN = 3 * QA_W + 3 * QB_W + 2 * D_MODEL

LANES = 128
COL_GATE_A = 0
COL_GATE_B = D_MODEL
COL_QA = 2 * D_MODEL
COL_KA = COL_QA + QA_W
COL_VA = COL_KA + QA_W
COL_QB = COL_VA + QA_W
COL_KB = COL_QB + QB_W
COL_VB = COL_KB + QB_W
TN = 512
N_TILES = N_IN // TN
VMEM_LIMIT = 56 * 1024 * 1024

F32 = jnp.float32
BF16 = jnp.bfloat16


def _alibi_slopes(n):
    return np.asarray(2.0 ** (-8.0 * (np.arange(n) + 1) / n), dtype=np.float32)


T_QA, T_VA, T_QB, T_KB, T_VB = COL_QA // TN, COL_VA // TN, COL_QB // TN, COL_KB // TN, COL_VB // TN
DILS = tuple(d for _, d in DIL_CONFIGS)


def _in_proj_kernel(x_ref, gmix_ref, w_ref, gvec_ref, gmat_ref, z_ref, g0_ref, g1_ref, g2_ref,
                    h_scr, t_scr, *, tm):
    j = pl.program_id(1)

    @pl.when(j == 0)
    def _():
        x = x_ref[...]
        ms = jnp.mean(x * x, axis=-1, keepdims=True)
        h_scr[...] = (x * lax.rsqrt(ms + NORM_EPS) * gmix_ref[...]).astype(BF16)

    def plain():
        return jnp.dot(h_scr[...], w_ref[...], preferred_element_type=F32)

    def normed():
        z = plain()
        ms = jnp.dot((z * z).astype(BF16), gmat_ref[...], preferred_element_type=F32)
        return z * lax.rsqrt(ms + NORM_EPS) * gvec_ref[...]

    @pl.when((j >= T_QA) & (j < T_VA))
    def _():
        z_ref[...] = normed().astype(BF16)

    @pl.when((j < T_QA) | ((j >= T_VA) & (j < T_QB)))
    def _():
        z_ref[...] = plain().astype(BF16)

    def scatter(g_ref, dil, val):
        if dil == 1:
            g_ref[0] = val.astype(BF16)
        else:
            for kb in range(TN // LANES):
                t_scr[kb] = val[:, kb * LANES:(kb + 1) * LANES]
            for c in range(dil):
                for kb in range(TN // LANES):
                    g_ref[c, :, kb * LANES:(kb + 1) * LANES] = (
                        t_scr[kb, pl.ds(c, tm // dil, stride=dil), :].astype(BF16))

    for g, (g_ref, dil) in enumerate(zip((g0_ref, g1_ref, g2_ref), DILS)):
        @pl.when((j == T_QB + g) | (j == T_KB + g))
        def _(g_ref=g_ref, dil=dil):
            scatter(g_ref, dil, normed())

        @pl.when(j == T_VB + g)
        def _(g_ref=g_ref, dil=dil):
            scatter(g_ref, dil, plain())


def _in_proj(x2, gmix, w_in_p, gvec, gmat, batch, seq, tm):
    T = x2.shape[0]
    tiles_per_seq = seq // tm
    main_w = COL_QB

    def group_spec(g, dil):
        def imap(i, j):
            sec = (j >= T_KB + g).astype(jnp.int32) + (j >= T_VB + g).astype(jnp.int32)
            return (sec, i // tiles_per_seq, 0, i % tiles_per_seq, 0)
        return pl.BlockSpec((None, None, dil, tm // dil, TN), imap)

    return pl.pallas_call(
        functools.partial(_in_proj_kernel, tm=tm),
        grid=(T // tm, N_TILES),
        in_specs=[
            pl.BlockSpec((tm, D_MODEL), lambda i, j: (i, 0)),
            pl.BlockSpec((1, D_MODEL), lambda i, j: (0, 0)),
            pl.BlockSpec((D_MODEL, TN), lambda i, j: (0, j)),
            pl.BlockSpec((1, TN), lambda i, j: (0, j)),
            pl.BlockSpec((None, TN, TN), lambda i, j: (jnp.where(j < T_QB, 0, 1), 0, 0)),
        ],
        out_specs=[pl.BlockSpec((tm, TN), lambda i, j: (i, jnp.minimum(j, T_QB - 1)))]
                  + [group_spec(g, dil) for g, dil in enumerate(DILS)],
        out_shape=[jax.ShapeDtypeStruct((T, main_w), BF16)]
                  + [jax.ShapeDtypeStruct((3, batch, dil, seq // dil, TN), BF16) for dil in DILS],
        scratch_shapes=[pltpu.VMEM((tm, D_MODEL), BF16), pltpu.VMEM((TN // LANES, tm, LANES), F32)],
        compiler_params=pltpu.CompilerParams(
            dimension_semantics=("arbitrary", "arbitrary"), vmem_limit_bytes=VMEM_LIMIT),
        name="in_proj",
    )(x2, gmix, w_in_p, gvec, gmat)


LOG2E = math.log2(math.e)
AUG_P, AUG_J, AUG_R = 0, 3, 6
REF_HEADROOM = 60.0
BOUND_LIMIT = 80.0


def _split3(x):
    def top(a):
        bits = lax.bitcast_convert_type(a, jnp.uint32) & jnp.uint32(0xFFFF0000)
        return lax.bitcast_convert_type(bits, F32)
    hi = top(x)
    mid = top(x - hi)
    lo = top(x - hi - mid)
    return hi, mid, lo


def _key_table(tk):
    def top(a):
        return (a.view(np.uint32) & np.uint32(0xFFFF0000)).view(np.float32)
    x = (np.arange(tk, dtype=np.float32) * np.float32(LOG2E)).astype(np.float32)
    hi = top(x)
    mid = top(x - hi)
    lo = top(x - hi - mid)
    tab = np.zeros((tk, LANES), np.float32)
    tab[:, AUG_P:AUG_P + 3] = 1.0
    tab[:, AUG_J], tab[:, AUG_J + 1], tab[:, AUG_J + 2] = hi, mid, lo
    tab[:, AUG_R] = 1.0
    return jnp.asarray(tab).astype(BF16)


def _diff_attn_kernel(slopes_ref, lamv_ref, q_ref, k_ref, v_ref, ktab_ref, gsub_ref, o_ref,
                      kcat, vcat, kstat, acc, m_scr, qpos, own_bias, *, tq, tk, seq, lam_init):
    h = pl.program_id(1)
    i = pl.program_id(2)
    n_chunks = seq // tk
    slope = slopes_ref[h]
    cs = slope * LOG2E
    nt = (((1,), (1,)), ((), ()))

    @pl.when(i == 0)
    def _():
        lane_k = lax.broadcasted_iota(jnp.int32, (tk, LANES), 1)
        ones_col = jnp.where(lane_k == 0, 1.0, 0.0).astype(BF16)

        def fill(c, carry):
            n1, n2 = carry
            sl = pl.ds(pl.multiple_of(c * tk, tk), tk)
            kc = k_ref[sl, :]
            kcat[sl, 0:LANES] = kc
            kcat[sl, LANES:2 * LANES] = ktab_ref[...]
            vcat[sl, 0:LANES] = v_ref[sl, :]
            vcat[sl, LANES:2 * LANES] = ones_col
            kf = kc.astype(F32)
            kk = kf * kf
            s1 = jnp.sum(jnp.where(lane_k < HD_A, kk, 0.0), axis=-1, keepdims=True)
            s2 = jnp.sum(jnp.where(lane_k >= HD_A, kk, 0.0), axis=-1, keepdims=True)
            return jnp.maximum(n1, s1), jnp.maximum(n2, s2)

        z = jnp.zeros((tk, 1), F32)
        n1, n2 = lax.fori_loop(0, n_chunks, fill, (z, z))
        kstat[0:1, :] = jnp.broadcast_to(jnp.sqrt(jnp.max(n1, axis=0, keepdims=True)), (1, LANES))
        kstat[1:2, :] = jnp.broadcast_to(jnp.sqrt(jnp.max(n2, axis=0, keepdims=True)), (1, LANES))

        row2 = lax.broadcasted_iota(jnp.int32, (2 * tq, LANES), 0)
        lane2 = lax.broadcasted_iota(jnp.int32, (2 * tq, LANES), 1)
        hi, mid, lo = _split3(cs * jnp.where(row2 >= tq, row2 - tq, row2).astype(F32))
        ppos = jnp.where(lane2 == AUG_P, hi, jnp.where(lane2 == AUG_P + 1, mid,
                                                       jnp.where(lane2 == AUG_P + 2, lo, 0.0)))
        qpos[...] = jnp.where((lane2 >= AUG_J) & (lane2 < AUG_J + 3), slope, 0.0) - ppos
        row = lax.broadcasted_iota(jnp.int32, (2 * tq, tq), 0)
        col = lax.broadcasted_iota(jnp.int32, (2 * tq, tq), 1)
        own_bias[...] = cs * jnp.abs(jnp.where(row >= tq, row - tq, row) - col).astype(F32)

    q = q_ref[...]
    lane = lax.broadcasted_iota(jnp.int32, (tq, LANES), 1)
    qf = q.astype(F32)
    qq = qf * qf
    qn1 = jnp.sqrt(jnp.sum(jnp.where(lane < HD_A, qq, 0.0), axis=-1, keepdims=True))
    qn2 = jnp.sqrt(jnp.sum(jnp.where(lane >= HD_A, qq, 0.0), axis=-1, keepdims=True))
    bound = jnp.concatenate([qn1 * kstat[0:1, 0:1], qn2 * kstat[1:2, 0:1]], axis=0) * 1.01
    bound_ok = jnp.max(bound) <= BOUND_LIMIT

    zero = jnp.zeros_like(q)
    base = jnp.concatenate([jnp.where(lane < HD_A, q, zero), jnp.where(lane >= HD_A, q, zero)], axis=0)
    lane2 = lax.broadcasted_iota(jnp.int32, (2 * tq, LANES), 1)
    rcol = jnp.where(lane2 == AUG_R, REF_HEADROOM - bound, 0.0)
    q_left = jnp.concatenate([base, (rcol + qpos[...]).astype(BF16)], axis=1)
    q_right = jnp.concatenate([base, (rcol - qpos[...]).astype(BF16)], axis=1)
    q_diag = jnp.concatenate([base, rcol.astype(BF16)], axis=1)
    q_plain = jnp.concatenate([base, jnp.zeros_like(base)], axis=1)
    i0 = i * tq
    cd = i0 // tk

    def chunk(c):
        return pl.ds(pl.multiple_of(c * tk, tk), tk)

    def fixed_reference():
        acc[...] = jnp.zeros(acc.shape, F32)

        def folded(keys, left, base):
            off = cs * (i0 - base).astype(F32)
            s = lax.dot_general(jnp.where(left, q_left, q_right), kcat[keys, :], nt,
                                preferred_element_type=F32) + jnp.where(left, -off, off)
            acc[...] += jnp.dot(jnp.exp2(s).astype(BF16), vcat[keys, :], preferred_element_type=F32)

        def body(c, carry):
            cc = c + (c >= cd).astype(jnp.int32)
            folded(chunk(cc), cc < cd, cc * tk)
            return carry

        n_main = n_chunks - 1
        lax.fori_loop(0, n_main, body, 0, unroll=next(u for u in (5, 3, 2, 1) if n_main % u == 0))

        own = (i0 - cd * tk) // tq
        s = lax.dot_general(q_diag, kcat[pl.ds(pl.multiple_of(i0, tq), tq), :], nt, preferred_element_type=F32)
        s = s - own_bias[...]
        acc[...] += jnp.dot(jnp.exp2(s).astype(BF16), vcat[pl.ds(pl.multiple_of(i0, tq), tq), :],
                            preferred_element_type=F32)
        for k in range(1, tk // tq):
            blk = (own + k) % (tk // tq)
            keys = pl.ds(pl.multiple_of(cd * tk + blk * tq, tq), tq)
            folded(keys, blk < own, cd * tk)

    def online_reference():
        acc[...] = jnp.zeros(acc.shape, F32)
        m_scr[...] = jnp.full(m_scr.shape, NEG_INF, F32)
        row = lax.broadcasted_iota(jnp.int32, (2 * tq, tq), 0)
        col = lax.broadcasted_iota(jnp.int32, (2 * tq, tq), 1)
        rel_own = (jnp.where(row >= tq, row - tq, row) - col).astype(F32)

        def body(c, carry):
            keys = pl.ds(pl.multiple_of(c * tq, tq), tq)
            s = lax.dot_general(q_plain, kcat[keys, :], nt, preferred_element_type=F32)
            s = s - cs * jnp.abs(rel_own + (i0 - c * tq).astype(F32))
            m_prev = m_scr[...]
            m_new = jnp.maximum(m_prev, jnp.max(s, axis=-1, keepdims=True))
            p = jnp.exp2(s - m_new)
            acc[...] = jnp.exp2(m_prev - m_new) * acc[...] + jnp.dot(
                p.astype(BF16), vcat[keys, :], preferred_element_type=F32)
            m_scr[...] = m_new
            return carry

        lax.fori_loop(0, seq // tq, body, 0)

    lax.cond(bound_ok, fixed_reference, online_reference)

    lamv = lamv_ref[...]
    lam = (jnp.exp(jnp.sum(lamv[0:1] * lamv[1:2], axis=-1, keepdims=True))
           - jnp.exp(jnp.sum(lamv[2:3] * lamv[3:4], axis=-1, keepdims=True)) + lam_init)
    a = acc[...]
    o = a[:, 0:LANES] / a[:, LANES:LANES + 1]
    d = o[:tq] - lam * o[tq:]
    ms = jnp.mean(d * d, axis=-1, keepdims=True)
    o_ref[...] = (d * lax.rsqrt(ms + SUBLN_EPS) * gsub_ref[...] * (1.0 - lam_init)).astype(BF16)


def _diff_attn(z3, slopes_a, lamv, gsub, lam_init, tq, tk):
    B, S, _ = z3.shape
    assert tk % tq == 0 and S % tk == 0
    qb0, kb0, vb0 = COL_QA // LANES, COL_KA // LANES, COL_VA // LANES
    kernel = functools.partial(_diff_attn_kernel, tq=tq, tk=tk, seq=S, lam_init=lam_init)
    return pl.pallas_call(
        kernel,
        grid=(B, H_A, S // tq),
        in_specs=[
            pl.BlockSpec(memory_space=pltpu.SMEM),
            pl.BlockSpec((4, HD_A), lambda b, h, i: (0, 0)),
            pl.BlockSpec((None, tq, LANES), lambda b, h, i: (b, i, qb0 + h)),
            pl.BlockSpec((None, S, LANES), lambda b, h, i: (b, 0, kb0 + h)),
            pl.BlockSpec((None, S, LANES), lambda b, h, i: (b, 0, vb0 + h)),
            pl.BlockSpec((tk, LANES), lambda b, h, i: (0, 0)),
            pl.BlockSpec((1, LANES), lambda b, h, i: (0, 0)),
        ],
        out_specs=pl.BlockSpec((None, tq, LANES), lambda b, h, i: (b, i, h)),
        out_shape=jax.ShapeDtypeStruct((B, S, OA_W), BF16),
        scratch_shapes=[pltpu.VMEM((S, 2 * LANES), BF16), pltpu.VMEM((S, 2 * LANES), BF16),
                        pltpu.VMEM((8, LANES), F32), pltpu.VMEM((2 * tq, 2 * LANES), F32),
                        pltpu.VMEM((2 * tq, 1), F32), pltpu.VMEM((2 * tq, LANES), F32),
                        pltpu.VMEM((2 * tq, tq), F32)],
        compiler_params=pltpu.CompilerParams(
            dimension_semantics=("arbitrary", "arbitrary", "arbitrary"), vmem_limit_bytes=VMEM_LIMIT),
        name="diff_attn",
    )(slopes_a, lamv, z3, z3, z3, _key_table(tk), gsub)


def _band_attn_kernel(q_ref, kp_ref, kc_ref, kn_ref, vp_ref, vc_ref, vn_ref, o_ref, lse_ref,
                      *, length, dil, half, slopes):
    i = pl.program_id(1)
    qb = Q_BLOCK
    kw = qb + 2 * half
    nrow = H_B * qb
    row = lax.broadcasted_iota(jnp.int32, (nrow, kw), 0)
    col = lax.broadcasted_iota(jnp.int32, (nrow, kw), 1)
    rel = col - half - (row & (qb - 1))
    kpos = i * qb - half + col
    valid = (jnp.abs(rel) <= half) & (kpos >= 0) & (kpos < length)
    slope_rows = jnp.full((nrow, kw), float(slopes[0]), F32)
    for hh in range(1, H_B):
        slope_rows = jnp.where(row >= hh * qb, float(slopes[hh]), slope_rows)
    bias = slope_rows * (dil * jnp.abs(rel)).astype(F32)
    lane_head = lax.broadcasted_iota(jnp.int32, (qb, TN), 1) // HD_B
    nt = (((1,), (1,)), ((), ()))

    def one_class(c, carry):
        rows = pl.ds(c, qb, stride=dil) if dil > 1 else pl.ds(0, qb)
        q4 = q_ref[c]
        zero = jnp.zeros_like(q4)
        qbd = jnp.concatenate([jnp.where(lane_head == hh, q4, zero) for hh in range(H_B)], axis=0)
        kk = jnp.concatenate([kp_ref[c, qb - half:qb, :], kc_ref[c], kn_ref[c, 0:half, :]], axis=0)
        vv = jnp.concatenate([vp_ref[c, qb - half:qb, :], vc_ref[c], vn_ref[c, 0:half, :]], axis=0)
        s = lax.dot_general(qbd, kk, nt, preferred_element_type=F32)
        s = jnp.where(valid, s - bias, NEG_INF)
        m = jnp.max(s, axis=-1, keepdims=True)
        p = jnp.exp(s - m)
        l = jnp.sum(p, axis=-1, keepdims=True)
        o_all = jnp.dot(p.astype(BF16), vv, preferred_element_type=F32)
        lse = m + jnp.log(l)
        for hh in range(H_B):
            rr = slice(hh * qb, (hh + 1) * qb)
            o_ref[hh, rows, :] = o_all[rr, hh * HD_B:(hh + 1) * HD_B] / l[rr]
            lse_ref[hh, rows, :] = jnp.broadcast_to(lse[rr], (qb, HD_B))
        return carry

    lax.fori_loop(0, dil, one_class, 0, unroll=2 if dil % 2 == 0 else 1)


def _band_attn(qkv, g, window, dil):
    _, B, _, L, _ = qkv.shape
    S = L * dil
    half = (window // 2) // dil
    nb = L // Q_BLOCK
    slopes = _alibi_slopes(N_GROUPS_B * H_B).reshape(N_GROUPS_B, H_B)[g]
    kernel = functools.partial(_band_attn_kernel, length=L, dil=dil, half=half, slopes=slopes)

    def spec(sec, shift):
        def imap(b, i):
            return (sec, b, 0, jnp.clip(i + shift, 0, nb - 1), 0)
        return pl.BlockSpec((None, None, dil, Q_BLOCK, TN), imap)

    out_spec = pl.BlockSpec((None, H_B, Q_BLOCK * dil, HD_B), lambda b, i: (b, 0, i, 0))
    out_shape = jax.ShapeDtypeStruct((B, H_B, S, HD_B), F32)
    return pl.pallas_call(
        kernel,
        grid=(B, nb),
        in_specs=[spec(0, 0), spec(1, -1), spec(1, 0), spec(1, 1), spec(2, -1), spec(2, 0), spec(2, 1)],
        out_specs=[out_spec, out_spec],
        out_shape=[out_shape, out_shape],
        compiler_params=pltpu.CompilerParams(
            dimension_semantics=("arbitrary", "arbitrary"), vmem_limit_bytes=VMEM_LIMIT),
        name=f"band_attn_g{g}",
    )(qkv, qkv, qkv, qkv, qkv, qkv, qkv)


def _merge_kernel(x_ref, ga_ref, gb_ref, oa_ref, o0_ref, o1_ref, o2_ref, l0_ref, l1_ref, l2_ref,
                  wpa_ref, wpb_ref, wo_ref, gffn_ref, x1_ref, h2_ref):
    heads = []
    for hh in range(H_B):
        l0, l1, l2 = l0_ref[hh], l1_ref[hh], l2_ref[hh]
        m = jnp.maximum(jnp.maximum(l0, l1), l2)
        w0, w1, w2 = jnp.exp(l0 - m), jnp.exp(l1 - m), jnp.exp(l2 - m)
        heads.append(((w0 * o0_ref[hh] + w1 * o1_ref[hh] + w2 * o2_ref[hh]) / (w0 + w1 + w2)).astype(BF16))
    ob = jnp.concatenate(heads, axis=1)
    pa = jnp.dot(oa_ref[...], wpa_ref[...], preferred_element_type=F32)
    pb = jnp.dot(ob, wpb_ref[...], preferred_element_type=F32)
    sigmoid = lambda g: 0.5 * jnp.tanh(0.5 * g.astype(F32)) + 0.5
    merged = sigmoid(ga_ref[...]) * pa + sigmoid(gb_ref[...]) * pb
    x1 = x_ref[...] + jnp.dot(merged.astype(BF16), wo_ref[...], preferred_element_type=F32)
    x1_ref[...] = x1
    ms = jnp.mean(x1 * x1, axis=-1, keepdims=True)
    h2_ref[...] = (x1 * lax.rsqrt(ms + NORM_EPS) * gffn_ref[...]).astype(BF16)


def _merge(x2, z2, oa2, obs, lses, wpa, wpb, wo, gffn, seq, tm):
    T = x2.shape[0]
    tiles_per_seq = seq // tm
    row = lambda w: pl.BlockSpec((tm, w), lambda i: (i, 0))
    per_head = pl.BlockSpec((None, H_B, tm, HD_B), lambda i: (i // tiles_per_seq, 0, i % tiles_per_seq, 0))
    const = lambda a: pl.BlockSpec(a.shape, lambda i: (0, 0), pipeline_mode=pl.Buffered(1))
    return pl.pallas_call(
        _merge_kernel,
        grid=(T // tm,),
        in_specs=[row(D_MODEL),
                  pl.BlockSpec((tm, D_MODEL), lambda i: (i, COL_GATE_A // D_MODEL)),
                  pl.BlockSpec((tm, D_MODEL), lambda i: (i, COL_GATE_B // D_MODEL)),
                  row(OA_W)] + [per_head] * 6 + [const(wpa), const(wpb), const(wo), const(gffn)],
        out_specs=[row(D_MODEL), row(D_MODEL)],
        out_shape=[jax.ShapeDtypeStruct((T, D_MODEL), F32), jax.ShapeDtypeStruct((T, D_MODEL), BF16)],
        compiler_params=pltpu.CompilerParams(
            dimension_semantics=("parallel",), vmem_limit_bytes=VMEM_LIMIT),
        name="merge",
    )(x2, z2, z2, oa2, *obs, *lses, wpa, wpb, wo, gffn)


HALO = 16


def _ffn_kernel(h_ref, hp_ref, hn_ref, x1_ref, wg_ref, wv_ref, cwg_ref, cwv_ref, cbg_ref, cbv_ref,
                wd_ref, o_ref, hcat_scr, acc_scr, *, tm, tiles_per_seq, n_f):
    i = pl.program_id(0)
    f = pl.program_id(1)

    @pl.when(f == 0)
    def _():
        pos = i % tiles_per_seq
        hp = hp_ref[...]
        hn = hn_ref[...]
        hcat_scr[0:HALO, :] = jnp.where(pos == 0, jnp.zeros_like(hp), hp)
        hcat_scr[HALO:HALO + tm, :] = h_ref[...]
        hcat_scr[HALO + tm:, :] = jnp.where(pos == tiles_per_seq - 1, jnp.zeros_like(hn), hn)
        acc_scr[...] = jnp.zeros(acc_scr.shape, F32)

    hcat = hcat_scr[...]
    rows = tm + 2 * HALO

    def conv_half(w_ref, cw_ref, cb_ref):
        a = jnp.dot(hcat, w_ref[...], preferred_element_type=F32)
        a_prev = pltpu.roll(a, 1, 0)[HALO:HALO + tm]
        a_next = pltpu.roll(a, rows - 1, 0)[HALO:HALO + tm]
        cw = cw_ref[...]
        return a_prev * cw[0:1] + a[HALO:HALO + tm] * cw[1:2] + a_next * cw[2:3] + cb_ref[...]

    ug = conv_half(wg_ref, cwg_ref, cbg_ref)
    uv = conv_half(wv_ref, cwv_ref, cbv_ref)
    act = 0.5 * ug * (1.0 + lax.erf(ug * np.float32(math.sqrt(0.5)))) * uv
    acc_scr[...] += jnp.dot(act.astype(BF16), wd_ref[...], preferred_element_type=F32)

    @pl.when(f == n_f - 1)
    def _():
        o_ref[...] = x1_ref[...] + acc_scr[...]


def _ffn(h2, x1, w_up, conv_w, conv_b, w_down, seq, tm, tf):
    T = h2.shape[0]
    n_f = D_FF // tf
    hb = tm // HALO
    n_hb = T // HALO
    kernel = functools.partial(_ffn_kernel, tm=tm, tiles_per_seq=seq // tm, n_f=n_f)
    return pl.pallas_call(
        kernel,
        grid=(T // tm, n_f),
        in_specs=[
            pl.BlockSpec((tm, D_MODEL), lambda i, f: (i, 0)),
            pl.BlockSpec((HALO, D_MODEL), lambda i, f: (jnp.maximum(i * hb - 1, 0), 0)),
            pl.BlockSpec((HALO, D_MODEL), lambda i, f: (jnp.minimum((i + 1) * hb, n_hb - 1), 0)),
            pl.BlockSpec((tm, D_MODEL), lambda i, f: (i, 0)),
            pl.BlockSpec((D_MODEL, tf), lambda i, f: (0, f)),
            pl.BlockSpec((D_MODEL, tf), lambda i, f: (0, f + n_f)),
            pl.BlockSpec((3, tf), lambda i, f: (0, f)),
            pl.BlockSpec((3, tf), lambda i, f: (0, f + n_f)),
            pl.BlockSpec((1, tf), lambda i, f: (0, f)),
            pl.BlockSpec((1, tf), lambda i, f: (0, f + n_f)),
            pl.BlockSpec((tf, D_MODEL), lambda i, f: (f, 0)),
        ],
        out_specs=pl.BlockSpec((tm, D_MODEL), lambda i, f: (i, 0)),
        out_shape=jax.ShapeDtypeStruct((T, D_MODEL), F32),
        scratch_shapes=[pltpu.VMEM((tm + 2 * HALO, D_MODEL), BF16), pltpu.VMEM((tm, D_MODEL), F32)],
        compiler_params=pltpu.CompilerParams(
            dimension_semantics=("parallel", "arbitrary"), vmem_limit_bytes=VMEM_LIMIT),
        name="ffn",
    )(h2, h2, h2, x1, w_up, w_up, conv_w, conv_w, conv_b, conv_b, w_down)


def _prep_params(g_mix_norm, w_in, g_qa, g_ka, lam_q1, lam_k1, lam_q2, lam_k2, g_subln, g_qb, g_kb,
                 w_pa, w_pb, w_o, g_ffn_norm, w_up, conv_w, conv_b, w_down):
    qa, ka, va, qb, kb, vb, gate_a, gate_b = jnp.split(
        w_in, list(np.cumsum([QA_W, QA_W, QA_W, QB_W, QB_W, QB_W, D_MODEL])), axis=-1)
    w_in_p = jnp.concatenate([gate_a, gate_b, qa, ka, va, qb, kb, vb], axis=-1).astype(BF16)
    ones = lambda n: jnp.ones((n,), F32)
    gvec = jnp.concatenate([
        ones(2 * D_MODEL),
        jnp.tile(g_qa.astype(F32), 2 * H_A) * (HD_A ** -0.5 * LOG2E),
        jnp.tile(g_ka.astype(F32), 2 * H_A),
        ones(QA_W),
        jnp.tile(g_qb.astype(F32), N_GROUPS_B * H_B) * (HD_B ** -0.5),
        jnp.tile(g_kb.astype(F32), N_GROUPS_B * H_B),
        ones(QB_W)]).reshape(1, N_IN)
    idx = np.arange(TN)
    gmat = jnp.asarray(np.stack([
        (idx[:, None] // HD_A == idx[None, :] // HD_A) / HD_A,
        (idx[:, None] // HD_B == idx[None, :] // HD_B) / HD_B]).astype(np.float32), BF16)
    lamv = jnp.stack([lam_q1, lam_k1, lam_q2, lam_k2]).astype(F32)
    return dict(
        gmix=g_mix_norm.astype(F32).reshape(1, D_MODEL), w_in=w_in_p, gvec=gvec, gmat=gmat, lamv=lamv,
        gsub=g_subln.astype(F32).reshape(1, 2 * HD_A), wpa=w_pa.astype(BF16), wpb=w_pb.astype(BF16),
        wo=w_o.astype(BF16), gffn=g_ffn_norm.astype(F32).reshape(1, D_MODEL), w_up=w_up.astype(BF16),
        conv_w=conv_w.astype(F32), conv_b=conv_b.astype(F32).reshape(1, 2 * D_FF),
        w_down=w_down.astype(BF16))


def _tiles(seq):
    return dict(in_proj_tm=min(1024, seq), attn_tq=min(512, seq // 2), attn_tk=min(1024, seq // 2),
                merge_tm=256, ffn_tm=512, ffn_tf=512)


def _encoder_layer(x, layer_idx, p):
    B, S, _ = x.shape
    T = B * S
    t = _tiles(S)
    lam_init = 0.8 - 0.6 * math.exp(-0.3 * layer_idx)
    x2 = x.reshape(T, D_MODEL)
    z2, *groups = _in_proj(x2, p["gmix"], p["w_in"], p["gvec"], p["gmat"], batch=B, seq=S,
                           tm=t["in_proj_tm"])
    z3 = z2.reshape(B, S, COL_QB)
    oa = _diff_attn(z3, jnp.asarray(_alibi_slopes(H_A)), p["lamv"], p["gsub"], lam_init,
                    tq=t["attn_tq"], tk=t["attn_tk"])
    obs, lses = [], []
    for g, (window, dil) in enumerate(DIL_CONFIGS):
        o, lse = _band_attn(groups[g], g, window, dil)
        obs.append(o)
        lses.append(lse)
    x1, h2 = _merge(x2, z2, oa.reshape(T, OA_W), obs, lses, p["wpa"], p["wpb"], p["wo"], p["gffn"],
                    seq=S, tm=t["merge_tm"])
    y = _ffn(h2, x1, p["w_up"], p["conv_w"], p["conv_b"], p["w_down"], seq=S, tm=t["ffn_tm"], tf=t["ffn_tf"])
    return y.reshape(B, S, D_MODEL)


def kernel(x_prompt, x_sample, g_mix_norm, w_in, g_qa, g_ka, lam_q1, lam_k1, lam_q2, lam_k2, g_subln,
           g_qb, g_kb, w_pa, w_pb, w_o, g_ffn_norm, w_up, conv_w, conv_b, w_down):
    y_prompt, y_sample = x_prompt, x_sample
    for l in range(w_in.shape[0]):
        p = _prep_params(g_mix_norm[l], w_in[l], g_qa[l], g_ka[l], lam_q1[l], lam_k1[l], lam_q2[l],
                         lam_k2[l], g_subln[l], g_qb[l], g_kb[l], w_pa[l], w_pb[l], w_o[l],
                         g_ffn_norm[l], w_up[l], conv_w[l], conv_b[l], w_down[l])
        y_prompt = _encoder_layer(y_prompt, l, p)
        y_sample = _encoder_layer(y_sample, l, p)
    return (y_prompt, y_sample)
```

```python
import functools
import math

import numpy as np
import jax
import jax.numpy as jnp
from jax import lax
from jax.experimental import pallas as pl
from jax.experimental.pallas import tpu as pltpu

D_MODEL = 2048
H_A = 8
HD_A = 64
DIL_CONFIGS = ((128, 1), (512, 4), (2048, 16))
N_GROUPS_B = 3
H_B = 4
HD_B = 128
D_FF = 5632
Q_BLOCK = 128
NORM_EPS = 1e-6
SUBLN_EPS = 1e-5
NEG_INF = -1e30

QA_W = H_A * 2 * HD_A
QB_W = N_GROUPS_B * H_B * HD_B
OA_W = QA_W
OB_W = H_B * HD_B
N_IN = 3 * QA_W + 3 * QB_W + 2 * D_MODEL

LANES = 128
COL_GATE_A = 0
COL_GATE_B = D_MODEL
COL_QA = 2 * D_MODEL
COL_KA = COL_QA + QA_W
COL_VA = COL_KA + QA_W
COL_QB = COL_VA + QA_W
COL_KB = COL_QB + QB_W
COL_VB = COL_KB + QB_W
TN = 512
N_TILES = N_IN // TN
VMEM_LIMIT = 56 * 1024 * 1024

F32 = jnp.float32
BF16 = jnp.bfloat16


def _alibi_slopes(n):
    return np.asarray(2.0 ** (-8.0 * (np.arange(n) + 1) / n), dtype=np.float32)


T_QA, T_VA, T_QB, T_KB, T_VB = COL_QA // TN, COL_VA // TN, COL_QB // TN, COL_KB // TN, COL_VB // TN
DILS = tuple(d for _, d in DIL_CONFIGS)


def _in_proj_kernel(x_ref, gmix_ref, w_ref, gvec_ref, gmat_ref, z_ref, g0_ref, g1_ref, g2_ref,
                    h_scr, t_scr, *, tm):
    j = pl.program_id(1)

    @pl.when(j == 0)
    def _():
        x = x_ref[...]
        ms = jnp.mean(x * x, axis=-1, keepdims=True)
        h_scr[...] = (x * lax.rsqrt(ms + NORM_EPS) * gmix_ref[...]).astype(BF16)

    def plain():
        return jnp.dot(h_scr[...], w_ref[...], preferred_element_type=F32)

    def normed():
        z = plain()
        ms = jnp.dot((z * z).astype(BF16), gmat_ref[...], preferred_element_type=F32)
        return z * lax.rsqrt(ms + NORM_EPS) * gvec_ref[...]

    @pl.when((j >= T_QA) & (j < T_VA))
    def _():
        z_ref[...] = normed().astype(BF16)

    @pl.when((j < T_QA) | ((j >= T_VA) & (j < T_QB)))
    def _():
        z_ref[...] = plain().astype(BF16)

    def scatter(g_ref, dil, val):
        if dil == 1:
            g_ref[0] = val.astype(BF16)
        else:
            for kb in range(TN // LANES):
                t_scr[kb] = val[:, kb * LANES:(kb + 1) * LANES]
            for c in range(dil):
                for kb in range(TN // LANES):
                    g_ref[c, :, kb * LANES:(kb + 1) * LANES] = (
                        t_scr[kb, pl.ds(c, tm // dil, stride=dil), :].astype(BF16))

    for g, (g_ref, dil) in enumerate(zip((g0_ref, g1_ref, g2_ref), DILS)):
        @pl.when((j == T_QB + g) | (j == T_KB + g))
        def _(g_ref=g_ref, dil=dil):
            scatter(g_ref, dil, normed())

        @pl.when(j == T_VB + g)
        def _(g_ref=g_ref, dil=dil):
            scatter(g_ref, dil, plain())


def _in_proj(x2, gmix, w_in_p, gvec, gmat, batch, seq, tm):
    T = x2.shape[0]
    tiles_per_seq = seq // tm
    main_w = COL_QB

    def group_spec(g, dil):
        def imap(i, j):
            sec = (j >= T_KB + g).astype(jnp.int32) + (j >= T_VB + g).astype(jnp.int32)
            return (sec, i // tiles_per_seq, 0, i % tiles_per_seq, 0)
        return pl.BlockSpec((None, None, dil, tm // dil, TN), imap)

    return pl.pallas_call(
        functools.partial(_in_proj_kernel, tm=tm),
        grid=(T // tm, N_TILES),
        in_specs=[
            pl.BlockSpec((tm, D_MODEL), lambda i, j: (i, 0)),
            pl.BlockSpec((1, D_MODEL), lambda i, j: (0, 0)),
            pl.BlockSpec((D_MODEL, TN), lambda i, j: (0, j)),
            pl.BlockSpec((1, TN), lambda i, j: (0, j)),
            pl.BlockSpec((None, TN, TN), lambda i, j: (jnp.where(j < T_QB, 0, 1), 0, 0)),
        ],
        out_specs=[pl.BlockSpec((tm, TN), lambda i, j: (i, jnp.minimum(j, T_QB - 1)))]
                  + [group_spec(g, dil) for g, dil in enumerate(DILS)],
        out_shape=[jax.ShapeDtypeStruct((T, main_w), BF16)]
                  + [jax.ShapeDtypeStruct((3, batch, dil, seq // dil, TN), BF16) for dil in DILS],
        scratch_shapes=[pltpu.VMEM((tm, D_MODEL), BF16), pltpu.VMEM((TN // LANES, tm, LANES), F32)],
        compiler_params=pltpu.CompilerParams(
            dimension_semantics=("arbitrary", "arbitrary"), vmem_limit_bytes=VMEM_LIMIT),
        name="in_proj",
    )(x2, gmix, w_in_p, gvec, gmat)


LOG2E = math.log2(math.e)
AUG_P, AUG_J, AUG_R = 0, 3, 6
REF_HEADROOM = 60.0
BOUND_LIMIT = 80.0


def _split3(x):
    def top(a):
        bits = lax.bitcast_convert_type(a, jnp.uint32) & jnp.uint32(0xFFFF0000)
        return lax.bitcast_convert_type(bits, F32)
    hi = top(x)
    mid = top(x - hi)
    lo = top(x - hi - mid)
    return hi, mid, lo


def _key_table(tk):
    def top(a):
        return (a.view(np.uint32) & np.uint32(0xFFFF0000)).view(np.float32)
    x = (np.arange(tk, dtype=np.float32) * np.float32(LOG2E)).astype(np.float32)
    hi = top(x)
    mid = top(x - hi)
    lo = top(x - hi - mid)
    tab = np.zeros((tk, LANES), np.float32)
    tab[:, AUG_P:AUG_P + 3] = 1.0
    tab[:, AUG_J], tab[:, AUG_J + 1], tab[:, AUG_J + 2] = hi, mid, lo
    tab[:, AUG_R] = 1.0
    return jnp.asarray(tab).astype(BF16)


def _diff_attn_kernel(slopes_ref, lamv_ref, q_ref, k_ref, v_ref, ktab_ref, gsub_ref, o_ref,
                      kcat, vcat, kstat, acc, m_scr, qpos, own_bias, *, tq, tk, seq, lam_init):
    h = pl.program_id(1)
    i = pl.program_id(2)
    n_chunks = seq // tk
    slope = slopes_ref[h]
    cs = slope * LOG2E
    nt = (((1,), (1,)), ((), ()))

    @pl.when(i == 0)
    def _():
        lane_k = lax.broadcasted_iota(jnp.int32, (tk, LANES), 1)
        ones_col = jnp.where(lane_k == 0, 1.0, 0.0).astype(BF16)

        def fill(c, carry):
            n1, n2 = carry
            sl = pl.ds(pl.multiple_of(c * tk, tk), tk)
            kc = k_ref[sl, :]
            kcat[sl, 0:LANES] = kc
            kcat[sl, LANES:2 * LANES] = ktab_ref[...]
            vcat[sl, 0:LANES] = v_ref[sl, :]
            vcat[sl, LANES:2 * LANES] = ones_col
            kf = kc.astype(F32)
            kk = kf * kf
            s1 = jnp.sum(jnp.where(lane_k < HD_A, kk, 0.0), axis=-1, keepdims=True)
            s2 = jnp.sum(jnp.where(lane_k >= HD_A, kk, 0.0), axis=-1, keepdims=True)
            return jnp.maximum(n1, s1), jnp.maximum(n2, s2)

        z = jnp.zeros((tk, 1), F32)
        n1, n2 = lax.fori_loop(0, n_chunks, fill, (z, z))
        kstat[0:1, :] = jnp.broadcast_to(jnp.sqrt(jnp.max(n1, axis=0, keepdims=True)), (1, LANES))
        kstat[1:2, :] = jnp.broadcast_to(jnp.sqrt(jnp.max(n2, axis=0, keepdims=True)), (1, LANES))

        row2 = lax.broadcasted_iota(jnp.int32, (2 * tq, LANES), 0)
        lane2 = lax.broadcasted_iota(jnp.int32, (2 * tq, LANES), 1)
        hi, mid, lo = _split3(cs * jnp.where(row2 >= tq, row2 - tq, row2).astype(F32))
        ppos = jnp.where(lane2 == AUG_P, hi, jnp.where(lane2 == AUG_P + 1, mid,
                                                       jnp.where(lane2 == AUG_P + 2, lo, 0.0)))
        qpos[...] = jnp.where((lane2 >= AUG_J) & (lane2 < AUG_J + 3), slope, 0.0) - ppos
        row = lax.broadcasted_iota(jnp.int32, (2 * tq, tq), 0)
        col = lax.broadcasted_iota(jnp.int32, (2 * tq, tq), 1)
        own_bias[...] = cs * jnp.abs(jnp.where(row >= tq, row - tq, row) - col).astype(F32)

    q = q_ref[...]
    lane = lax.broadcasted_iota(jnp.int32, (tq, LANES), 1)
    qf = q.astype(F32)
    qq = qf * qf
    qn1 = jnp.sqrt(jnp.sum(jnp.where(lane < HD_A, qq, 0.0), axis=-1, keepdims=True))
    qn2 = jnp.sqrt(jnp.sum(jnp.where(lane >= HD_A, qq, 0.0), axis=-1, keepdims=True))
    bound = jnp.concatenate([qn1 * kstat[0:1, 0:1], qn2 * kstat[1:2, 0:1]], axis=0) * 1.01
    bound_ok = jnp.max(bound) <= BOUND_LIMIT

    zero = jnp.zeros_like(q)
    base = jnp.concatenate([jnp.where(lane < HD_A, q, zero), jnp.where(lane >= HD_A, q, zero)], axis=0)
    lane2 = lax.broadcasted_iota(jnp.int32, (2 * tq, LANES), 1)
    rcol = jnp.where(lane2 == AUG_R, REF_HEADROOM - bound, 0.0)
    q_left = jnp.concatenate([base, (rcol + qpos[...]).astype(BF16)], axis=1)
    q_right = jnp.concatenate([base, (rcol - qpos[...]).astype(BF16)], axis=1)
    q_diag = jnp.concatenate([base, rcol.astype(BF16)], axis=1)
    q_plain = jnp.concatenate([base, jnp.zeros_like(base)], axis=1)
    i0 = i * tq
    cd = i0 // tk

    def chunk(c):
        return pl.ds(pl.multiple_of(c * tk, tk), tk)

    def fixed_reference():
        acc[...] = jnp.zeros(acc.shape, F32)

        def folded(keys, left, base):
            off = cs * (i0 - base).astype(F32)
            s = lax.dot_general(jnp.where(left, q_left, q_right), kcat[keys, :], nt,
                                preferred_element_type=F32) + jnp.where(left, -off, off)
            acc[...] += jnp.dot(jnp.exp2(s).astype(BF16), vcat[keys, :], preferred_element_type=F32)

        def body(c, carry):
            cc = c + (c >= cd).astype(jnp.int32)
            folded(chunk(cc), cc < cd, cc * tk)
            return carry

        n_main = n_chunks - 1
        lax.fori_loop(0, n_main, body, 0, unroll=next(u for u in (15, 5, 3, 2, 1) if n_main % u == 0))

        own = (i0 - cd * tk) // tq
        s = lax.dot_general(q_diag, kcat[pl.ds(pl.multiple_of(i0, tq), tq), :], nt, preferred_element_type=F32)
        s = s - own_bias[...]
        acc[...] += jnp.dot(jnp.exp2(s).astype(BF16), vcat[pl.ds(pl.multiple_of(i0, tq), tq), :],
                            preferred_element_type=F32)
        for k in range(1, tk // tq):
            blk = (own + k) % (tk // tq)
            keys = pl.ds(pl.multiple_of(cd * tk + blk * tq, tq), tq)
            folded(keys, blk < own, cd * tk)

    def online_reference():
        acc[...] = jnp.zeros(acc.shape, F32)
        m_scr[...] = jnp.full(m_scr.shape, NEG_INF, F32)
        row = lax.broadcasted_iota(jnp.int32, (2 * tq, tq), 0)
        col = lax.broadcasted_iota(jnp.int32, (2 * tq, tq), 1)
        rel_own = (jnp.where(row >= tq, row - tq, row) - col).astype(F32)

        def body(c, carry):
            keys = pl.ds(pl.multiple_of(c * tq, tq), tq)
            s = lax.dot_general(q_plain, kcat[keys, :], nt, preferred_element_type=F32)
            s = s - cs * jnp.abs(rel_own + (i0 - c * tq).astype(F32))
            m_prev = m_scr[...]
            m_new = jnp.maximum(m_prev, jnp.max(s, axis=-1, keepdims=True))
            p = jnp.exp2(s - m_new)
            acc[...] = jnp.exp2(m_prev - m_new) * acc[...] + jnp.dot(
                p.astype(BF16), vcat[keys, :], preferred_element_type=F32)
            m_scr[...] = m_new
            return carry

        lax.fori_loop(0, seq // tq, body, 0)

    lax.cond(bound_ok, fixed_reference, online_reference)

    lamv = lamv_ref[...]
    lam = (jnp.exp(jnp.sum(lamv[0:1] * lamv[1:2], axis=-1, keepdims=True))
           - jnp.exp(jnp.sum(lamv[2:3] * lamv[3:4], axis=-1, keepdims=True)) + lam_init)
    a = acc[...]
    o = a[:, 0:LANES] / a[:, LANES:LANES + 1]
    d = o[:tq] - lam * o[tq:]
    ms = jnp.mean(d * d, axis=-1, keepdims=True)
    o_ref[...] = (d * lax.rsqrt(ms + SUBLN_EPS) * gsub_ref[...] * (1.0 - lam_init)).astype(BF16)


def _diff_attn(z3, slopes_a, lamv, gsub, lam_init, tq, tk):
    B, S, _ = z3.shape
    assert tk % tq == 0 and S % tk == 0
    qb0, kb0, vb0 = COL_QA // LANES, COL_KA // LANES, COL_VA // LANES
    kernel = functools.partial(_diff_attn_kernel, tq=tq, tk=tk, seq=S, lam_init=lam_init)
    return pl.pallas_call(
        kernel,
        grid=(B, H_A, S // tq),
        in_specs=[
            pl.BlockSpec(memory_space=pltpu.SMEM),
            pl.BlockSpec((4, HD_A), lambda b, h, i: (0, 0)),
            pl.BlockSpec((None, tq, LANES), lambda b, h, i: (b, i, qb0 + h)),
            pl.BlockSpec((None, S, LANES), lambda b, h, i: (b, 0, kb0 + h)),
            pl.BlockSpec((None, S, LANES), lambda b, h, i: (b, 0, vb0 + h)),
            pl.BlockSpec((tk, LANES), lambda b, h, i: (0, 0)),
            pl.BlockSpec((1, LANES), lambda b, h, i: (0, 0)),
        ],
        out_specs=pl.BlockSpec((None, tq, LANES), lambda b, h, i: (b, i, h)),
        out_shape=jax.ShapeDtypeStruct((B, S, OA_W), BF16),
        scratch_shapes=[pltpu.VMEM((S, 2 * LANES), BF16), pltpu.VMEM((S, 2 * LANES), BF16),
                        pltpu.VMEM((8, LANES), F32), pltpu.VMEM((2 * tq, 2 * LANES), F32),
                        pltpu.VMEM((2 * tq, 1), F32), pltpu.VMEM((2 * tq, LANES), F32),
                        pltpu.VMEM((2 * tq, tq), F32)],
        compiler_params=pltpu.CompilerParams(
            dimension_semantics=("arbitrary", "arbitrary", "arbitrary"), vmem_limit_bytes=VMEM_LIMIT),
        name="diff_attn",
    )(slopes_a, lamv, z3, z3, z3, _key_table(tk), gsub)


def _band_attn_kernel(q_ref, kp_ref, kc_ref, kn_ref, vp_ref, vc_ref, vn_ref, o_ref, lse_ref,
                      *, length, dil, half, slopes):
    i = pl.program_id(1)
    qb = Q_BLOCK
    kw = qb + 2 * half
    nrow = H_B * qb
    row = lax.broadcasted_iota(jnp.int32, (nrow, kw), 0)
    col = lax.broadcasted_iota(jnp.int32, (nrow, kw), 1)
    rel = col - half - (row & (qb - 1))
    kpos = i * qb - half + col
    valid = (jnp.abs(rel) <= half) & (kpos >= 0) & (kpos < length)
    slope_rows = jnp.full((nrow, kw), float(slopes[0]), F32)
    for hh in range(1, H_B):
        slope_rows = jnp.where(row >= hh * qb, float(slopes[hh]), slope_rows)
    bias = slope_rows * (dil * jnp.abs(rel)).astype(F32)
    lane_head = lax.broadcasted_iota(jnp.int32, (qb, TN), 1) // HD_B
    nt = (((1,), (1,)), ((), ()))

    def one_class(c, carry):
        rows = pl.ds(c, qb, stride=dil) if dil > 1 else pl.ds(0, qb)
        q4 = q_ref[c]
        zero = jnp.zeros_like(q4)
        qbd = jnp.concatenate([jnp.where(lane_head == hh, q4, zero) for hh in range(H_B)], axis=0)
        kk = jnp.concatenate([kp_ref[c, qb - half:qb, :], kc_ref[c], kn_ref[c, 0:half, :]], axis=0)
        vv = jnp.concatenate([vp_ref[c, qb - half:qb, :], vc_ref[c], vn_ref[c, 0:half, :]], axis=0)
        s = lax.dot_general(qbd, kk, nt, preferred_element_type=F32)
        s = jnp.where(valid, s - bias, NEG_INF)
        m = jnp.max(s, axis=-1, keepdims=True)
        p = jnp.exp(s - m)
        l = jnp.sum(p, axis=-1, keepdims=True)
        o_all = jnp.dot(p.astype(BF16), vv, preferred_element_type=F32)
        lse = m + jnp.log(l)
        for hh in range(H_B):
            rr = slice(hh * qb, (hh + 1) * qb)
            o_ref[hh, rows, :] = o_all[rr, hh * HD_B:(hh + 1) * HD_B] / l[rr]
            lse_ref[hh, rows, :] = jnp.broadcast_to(lse[rr], (qb, HD_B))
        return carry

    lax.fori_loop(0, dil, one_class, 0, unroll=min(dil, 4))


def _band_attn(qkv, g, window, dil):
    _, B, _, L, _ = qkv.shape
    S = L * dil
    half = (window // 2) // dil
    nb = L // Q_BLOCK
    slopes = _alibi_slopes(N_GROUPS_B * H_B).reshape(N_GROUPS_B, H_B)[g]
    kernel = functools.partial(_band_attn_kernel, length=L, dil=dil, half=half, slopes=slopes)

    def spec(sec, shift):
        def imap(b, i):
            return (sec, b, 0, jnp.clip(i + shift, 0, nb - 1), 0)
        return pl.BlockSpec((None, None, dil, Q_BLOCK, TN), imap)

    out_spec = pl.BlockSpec((None, H_B, Q_BLOCK * dil, HD_B), lambda b, i: (b, 0, i, 0))
    out_shape = jax.ShapeDtypeStruct((B, H_B, S, HD_B), F32)
    return pl.pallas_call(
        kernel,
        grid=(B, nb),
        in_specs=[spec(0, 0), spec(1, -1), spec(1, 0), spec(1, 1), spec(2, -1), spec(2, 0), spec(2, 1)],
        out_specs=[out_spec, out_spec],
        out_shape=[out_shape, out_shape],
        compiler_params=pltpu.CompilerParams(
            dimension_semantics=("arbitrary", "arbitrary"), vmem_limit_bytes=VMEM_LIMIT),
        name=f"band_attn_g{g}",
    )(qkv, qkv, qkv, qkv, qkv, qkv, qkv)


def _merge_kernel(x_ref, ga_ref, gb_ref, oa_ref, o0_ref, o1_ref, o2_ref, l0_ref, l1_ref, l2_ref,
                  wpa_ref, wpb_ref, wo_ref, gffn_ref, x1_ref, h2_ref):
    heads = []
    for hh in range(H_B):
        l0, l1, l2 = l0_ref[hh], l1_ref[hh], l2_ref[hh]
        m = jnp.maximum(jnp.maximum(l0, l1), l2)
        w0, w1, w2 = jnp.exp(l0 - m), jnp.exp(l1 - m), jnp.exp(l2 - m)
        heads.append(((w0 * o0_ref[hh] + w1 * o1_ref[hh] + w2 * o2_ref[hh]) / (w0 + w1 + w2)).astype(BF16))
    ob = jnp.concatenate(heads, axis=1)
    pa = jnp.dot(oa_ref[...], wpa_ref[...], preferred_element_type=F32)
    pb = jnp.dot(ob, wpb_ref[...], preferred_element_type=F32)
    sigmoid = lambda g: 0.5 * jnp.tanh(0.5 * g.astype(F32)) + 0.5
    merged = sigmoid(ga_ref[...]) * pa + sigmoid(gb_ref[...]) * pb
    x1 = x_ref[...] + jnp.dot(merged.astype(BF16), wo_ref[...], preferred_element_type=F32)
    x1_ref[...] = x1
    ms = jnp.mean(x1 * x1, axis=-1, keepdims=True)
    h2_ref[...] = (x1 * lax.rsqrt(ms + NORM_EPS) * gffn_ref[...]).astype(BF16)


def _merge(x2, z2, oa2, obs, lses, wpa, wpb, wo, gffn, seq, tm):
    T = x2.shape[0]
    tiles_per_seq = seq // tm
    row = lambda w: pl.BlockSpec((tm, w), lambda i: (i, 0))
    per_head = pl.BlockSpec((None, H_B, tm, HD_B), lambda i: (i // tiles_per_seq, 0, i % tiles_per_seq, 0))
    const = lambda a: pl.BlockSpec(a.shape, lambda i: (0, 0), pipeline_mode=pl.Buffered(1))
    return pl.pallas_call(
        _merge_kernel,
        grid=(T // tm,),
        in_specs=[row(D_MODEL),
                  pl.BlockSpec((tm, D_MODEL), lambda i: (i, COL_GATE_A // D_MODEL)),
                  pl.BlockSpec((tm, D_MODEL), lambda i: (i, COL_GATE_B // D_MODEL)),
                  row(OA_W)] + [per_head] * 6 + [const(wpa), const(wpb), const(wo), const(gffn)],
        out_specs=[row(D_MODEL), row(D_MODEL)],
        out_shape=[jax.ShapeDtypeStruct((T, D_MODEL), F32), jax.ShapeDtypeStruct((T, D_MODEL), BF16)],
        compiler_params=pltpu.CompilerParams(
            dimension_semantics=("parallel",), vmem_limit_bytes=VMEM_LIMIT),
        name="merge",
    )(x2, z2, z2, oa2, *obs, *lses, wpa, wpb, wo, gffn)


HALO = 16


def _ffn_kernel(h_ref, hp_ref, hn_ref, x1_ref, wg_ref, wv_ref, cwg_ref, cwv_ref, cbg_ref, cbv_ref,
                wd_ref, o_ref, hcat_scr, acc_scr, *, tm, tiles_per_seq, n_f):
    i = pl.program_id(0)
    f = pl.program_id(1)

    @pl.when(f == 0)
    def _():
        pos = i % tiles_per_seq
        hp = hp_ref[...]
        hn = hn_ref[...]
        hcat_scr[0:HALO, :] = jnp.where(pos == 0, jnp.zeros_like(hp), hp)
        hcat_scr[HALO:HALO + tm, :] = h_ref[...]
        hcat_scr[HALO + tm:, :] = jnp.where(pos == tiles_per_seq - 1, jnp.zeros_like(hn), hn)
        acc_scr[...] = jnp.zeros(acc_scr.shape, F32)

    hcat = hcat_scr[...]
    rows = tm + 2 * HALO

    def conv_half(w_ref, cw_ref, cb_ref):
        a = jnp.dot(hcat, w_ref[...], preferred_element_type=F32)
        a_prev = pltpu.roll(a, 1, 0)[HALO:HALO + tm]
        a_next = pltpu.roll(a, rows - 1, 0)[HALO:HALO + tm]
        cw = cw_ref[...]
        return a_prev * cw[0:1] + a[HALO:HALO + tm] * cw[1:2] + a_next * cw[2:3] + cb_ref[...]

    ug = conv_half(wg_ref, cwg_ref, cbg_ref)
    uv = conv_half(wv_ref, cwv_ref, cbv_ref)
    act = 0.5 * ug * (1.0 + lax.erf(ug * np.float32(math.sqrt(0.5)))) * uv
    acc_scr[...] += jnp.dot(act.astype(BF16), wd_ref[...], preferred_element_type=F32)

    @pl.when(f == n_f - 1)
    def _():
        o_ref[...] = x1_ref[...] + acc_scr[...]


def _ffn(h2, x1, w_up, conv_w, conv_b, w_down, seq, tm, tf):
    T = h2.shape[0]
    n_f = D_FF // tf
    hb = tm // HALO
    n_hb = T // HALO
    kernel = functools.partial(_ffn_kernel, tm=tm, tiles_per_seq=seq // tm, n_f=n_f)
    return pl.pallas_call(
        kernel,
        grid=(T // tm, n_f),
        in_specs=[
            pl.BlockSpec((tm, D_MODEL), lambda i, f: (i, 0)),
            pl.BlockSpec((HALO, D_MODEL), lambda i, f: (jnp.maximum(i * hb - 1, 0), 0)),
            pl.BlockSpec((HALO, D_MODEL), lambda i, f: (jnp.minimum((i + 1) * hb, n_hb - 1), 0)),
            pl.BlockSpec((tm, D_MODEL), lambda i, f: (i, 0)),
            pl.BlockSpec((D_MODEL, tf), lambda i, f: (0, f)),
            pl.BlockSpec((D_MODEL, tf), lambda i, f: (0, f + n_f)),
            pl.BlockSpec((3, tf), lambda i, f: (0, f)),
            pl.BlockSpec((3, tf), lambda i, f: (0, f + n_f)),
            pl.BlockSpec((1, tf), lambda i, f: (0, f)),
            pl.BlockSpec((1, tf), lambda i, f: (0, f + n_f)),
            pl.BlockSpec((tf, D_MODEL), lambda i, f: (f, 0)),
        ],
        out_specs=pl.BlockSpec((tm, D_MODEL), lambda i, f: (i, 0)),
        out_shape=jax.ShapeDtypeStruct((T, D_MODEL), F32),
        scratch_shapes=[pltpu.VMEM((tm + 2 * HALO, D_MODEL), BF16), pltpu.VMEM((tm, D_MODEL), F32)],
        compiler_params=pltpu.CompilerParams(
            dimension_semantics=("parallel", "arbitrary"), vmem_limit_bytes=VMEM_LIMIT),
        name="ffn",
    )(h2, h2, h2, x1, w_up, w_up, conv_w, conv_w, conv_b, conv_b, w_down)


def _prep_params(g_mix_norm, w_in, g_qa, g_ka, lam_q1, lam_k1, lam_q2, lam_k2, g_subln, g_qb, g_kb,
                 w_pa, w_pb, w_o, g_ffn_norm, w_up, conv_w, conv_b, w_down):
    qa, ka, va, qb, kb, vb, gate_a, gate_b = jnp.split(
        w_in, list(np.cumsum([QA_W, QA_W, QA_W, QB_W, QB_W, QB_W, D_MODEL])), axis=-1)
    w_in_p = jnp.concatenate([gate_a, gate_b, qa, ka, va, qb, kb, vb], axis=-1).astype(BF16)
    ones = lambda n: jnp.ones((n,), F32)
    gvec = jnp.concatenate([
        ones(2 * D_MODEL),
        jnp.tile(g_qa.astype(F32), 2 * H_A) * (HD_A ** -0.5 * LOG2E),
        jnp.tile(g_ka.astype(F32), 2 * H_A),
        ones(QA_W),
        jnp.tile(g_qb.astype(F32), N_GROUPS_B * H_B) * (HD_B ** -0.5),
        jnp.tile(g_kb.astype(F32), N_GROUPS_B * H_B),
        ones(QB_W)]).reshape(1, N_IN)
    idx = np.arange(TN)
    gmat = jnp.asarray(np.stack([
        (idx[:, None] // HD_A == idx[None, :] // HD_A) / HD_A,
        (idx[:, None] // HD_B == idx[None, :] // HD_B) / HD_B]).astype(np.float32), BF16)
    lamv = jnp.stack([lam_q1, lam_k1, lam_q2, lam_k2]).astype(F32)
    return dict(
        gmix=g_mix_norm.astype(F32).reshape(1, D_MODEL), w_in=w_in_p, gvec=gvec, gmat=gmat, lamv=lamv,
        gsub=g_subln.astype(F32).reshape(1, 2 * HD_A), wpa=w_pa.astype(BF16), wpb=w_pb.astype(BF16),
        wo=w_o.astype(BF16), gffn=g_ffn_norm.astype(F32).reshape(1, D_MODEL), w_up=w_up.astype(BF16),
        conv_w=conv_w.astype(F32), conv_b=conv_b.astype(F32).reshape(1, 2 * D_FF),
        w_down=w_down.astype(BF16))


def _tiles(seq):
    return dict(in_proj_tm=min(1024, seq), attn_tq=min(512, seq // 2), attn_tk=min(1024, seq // 2),
                merge_tm=256, ffn_tm=512, ffn_tf=512)


def _encoder_layer(x, layer_idx, p):
    B, S, _ = x.shape
    T = B * S
    t = _tiles(S)
    lam_init = 0.8 - 0.6 * math.exp(-0.3 * layer_idx)
    x2 = x.reshape(T, D_MODEL)
    z2, *groups = _in_proj(x2, p["gmix"], p["w_in"], p["gvec"], p["gmat"], batch=B, seq=S,
                           tm=t["in_proj_tm"])
    z3 = z2.reshape(B, S, COL_QB)
    oa = _diff_attn(z3, jnp.asarray(_alibi_slopes(H_A)), p["lamv"], p["gsub"], lam_init,
                    tq=t["attn_tq"], tk=t["attn_tk"])
    obs, lses = [], []
    for g, (window, dil) in enumerate(DIL_CONFIGS):
        o, lse = _band_attn(groups[g], g, window, dil)
        obs.append(o)
        lses.append(lse)
    x1, h2 = _merge(x2, z2, oa.reshape(T, OA_W), obs, lses, p["wpa"], p["wpb"], p["wo"], p["gffn"],
                    seq=S, tm=t["merge_tm"])
    y = _ffn(h2, x1, p["w_up"], p["conv_w"], p["conv_b"], p["w_down"], seq=S, tm=t["ffn_tm"], tf=t["ffn_tf"])
    return y.reshape(B, S, D_MODEL)


def kernel(x_prompt, x_sample, g_mix_norm, w_in, g_qa, g_ka, lam_q1, lam_k1, lam_q2, lam_k2, g_subln,
           g_qb, g_kb, w_pa, w_pb, w_o, g_ffn_norm, w_up, conv_w, conv_b, w_down):
    y_prompt, y_sample = x_prompt, x_sample
    for l in range(w_in.shape[0]):
        p = _prep_params(g_mix_norm[l], w_in[l], g_qa[l], g_ka[l], lam_q1[l], lam_k1[l], lam_q2[l],
                         lam_k2[l], g_subln[l], g_qb[l], g_kb[l], w_pa[l], w_pb[l], w_o[l],
                         g_ffn_norm[l], w_up[l], conv_w[l], conv_b[l], w_down[l])
        y_prompt = _encoder_layer(y_prompt, l, p)
        y_sample = _encoder_layer(y_sample, l, p)
    return (y_prompt, y_sample)
```

```python
import functools
import math

import numpy as np
import jax
import jax.numpy as jnp
from jax import lax
from jax.experimental import pallas as pl
from jax.experimental.pallas import tpu as pltpu

D_MODEL = 2048
H_A = 8
HD_A = 64
DIL_CONFIGS = ((128, 1), (512, 4), (2048, 16))
N_GROUPS_B = 3
H_B = 4
HD_B = 128
D_FF = 5632
Q_BLOCK = 128
NORM_EPS = 1e-6
SUBLN_EPS = 1e-5
NEG_INF = -1e30

QA_W = H_A * 2 * HD_A
QB_W = N_GROUPS_B * H_B * HD_B
OA_W = QA_W
OB_W = H_B * HD_B
N_IN = 3 * QA_W + 3 * QB_W + 2 * D_MODEL

LANES = 128
COL_GATE_A = 0
COL_GATE_B = D_MODEL
COL_QA = 2 * D_MODEL
COL_KA = COL_QA + QA_W
COL_VA = COL_KA + QA_W
COL_QB = COL_VA + QA_W
COL_KB = COL_QB + QB_W
COL_VB = COL_KB + QB_W
TN = 512
N_TILES = N_IN // TN
VMEM_LIMIT = 56 * 1024 * 1024

F32 = jnp.float32
BF16 = jnp.bfloat16


def _alibi_slopes(n):
    return np.asarray(2.0 ** (-8.0 * (np.arange(n) + 1) / n), dtype=np.float32)


T_QA, T_VA, T_QB, T_KB, T_VB = COL_QA // TN, COL_VA // TN, COL_QB // TN, COL_KB // TN, COL_VB // TN
DILS = tuple(d for _, d in DIL_CONFIGS)


def _in_proj_kernel(x_ref, gmix_ref, w_ref, gvec_ref, gmat_ref, z_ref, g0_ref, g1_ref, g2_ref,
                    h_scr, t_scr, *, tm):
    j = pl.program_id(1)

    @pl.when(j == 0)
    def _():
        x = x_ref[...]
        ms = jnp.mean(x * x, axis=-1, keepdims=True)
        h_scr[...] = (x * lax.rsqrt(ms + NORM_EPS) * gmix_ref[...]).astype(BF16)

    def plain():
        return jnp.dot(h_scr[...], w_ref[...], preferred_element_type=F32)

    def normed():
        z = plain()
        ms = jnp.dot((z * z).astype(BF16), gmat_ref[...], preferred_element_type=F32)
        return z * lax.rsqrt(ms + NORM_EPS) * gvec_ref[...]

    @pl.when((j >= T_QA) & (j < T_VA))
    def _():
        z_ref[...] = normed().astype(BF16)

    @pl.when((j < T_QA) | ((j >= T_VA) & (j < T_QB)))
    def _():
        z_ref[...] = plain().astype(BF16)

    def scatter(g_ref, dil, val):
        if dil == 1:
            g_ref[0] = val.astype(BF16)
        else:
            for kb in range(TN // LANES):
                t_scr[kb] = val[:, kb * LANES:(kb + 1) * LANES]
            for c in range(dil):
                for kb in range(TN // LANES):
                    g_ref[c, :, kb * LANES:(kb + 1) * LANES] = (
                        t_scr[kb, pl.ds(c, tm // dil, stride=dil), :].astype(BF16))

    for g, (g_ref, dil) in enumerate(zip((g0_ref, g1_ref, g2_ref), DILS)):
        @pl.when((j == T_QB + g) | (j == T_KB + g))
        def _(g_ref=g_ref, dil=dil):
            scatter(g_ref, dil, normed())

        @pl.when(j == T_VB + g)
        def _(g_ref=g_ref, dil=dil):
            scatter(g_ref, dil, plain())


def _in_proj(x2, gmix, w_in_p, gvec, gmat, batch, seq, tm):
    T = x2.shape[0]
    tiles_per_seq = seq // tm
    main_w = COL_QB

    def group_spec(g, dil):
        def imap(i, j):
            sec = (j >= T_KB + g).astype(jnp.int32) + (j >= T_VB + g).astype(jnp.int32)
            return (sec, i // tiles_per_seq, 0, i % tiles_per_seq, 0)
        return pl.BlockSpec((None, None, dil, tm // dil, TN), imap)

    return pl.pallas_call(
        functools.partial(_in_proj_kernel, tm=tm),
        grid=(T // tm, N_TILES),
        in_specs=[
            pl.BlockSpec((tm, D_MODEL), lambda i, j: (i, 0)),
            pl.BlockSpec((1, D_MODEL), lambda i, j: (0, 0)),
            pl.BlockSpec((D_MODEL, TN), lambda i, j: (0, j)),
            pl.BlockSpec((1, TN), lambda i, j: (0, j)),
            pl.BlockSpec((None, TN, TN), lambda i, j: (jnp.where(j < T_QB, 0, 1), 0, 0)),
        ],
        out_specs=[pl.BlockSpec((tm, TN), lambda i, j: (i, jnp.minimum(j, T_QB - 1)))]
                  + [group_spec(g, dil) for g, dil in enumerate(DILS)],
        out_shape=[jax.ShapeDtypeStruct((T, main_w), BF16)]
                  + [jax.ShapeDtypeStruct((3, batch, dil, seq // dil, TN), BF16) for dil in DILS],
        scratch_shapes=[pltpu.VMEM((tm, D_MODEL), BF16), pltpu.VMEM((TN // LANES, tm, LANES), F32)],
        compiler_params=pltpu.CompilerParams(
            dimension_semantics=("arbitrary", "arbitrary"), vmem_limit_bytes=VMEM_LIMIT),
        name="in_proj",
    )(x2, gmix, w_in_p, gvec, gmat)


LOG2E = math.log2(math.e)
AUG_P, AUG_J, AUG_R = 0, 3, 6
REF_HEADROOM = 60.0
BOUND_LIMIT = 80.0


def _split3(x):
    def top(a):
        bits = lax.bitcast_convert_type(a, jnp.uint32) & jnp.uint32(0xFFFF0000)
        return lax.bitcast_convert_type(bits, F32)
    hi = top(x)
    mid = top(x - hi)
    lo = top(x - hi - mid)
    return hi, mid, lo


def _key_table(tk):
    def top(a):
        return (a.view(np.uint32) & np.uint32(0xFFFF0000)).view(np.float32)
    x = (np.arange(tk, dtype=np.float32) * np.float32(LOG2E)).astype(np.float32)
    hi = top(x)
    mid = top(x - hi)
    lo = top(x - hi - mid)
    tab = np.zeros((tk, LANES), np.float32)
    tab[:, AUG_P:AUG_P + 3] = 1.0
    tab[:, AUG_J], tab[:, AUG_J + 1], tab[:, AUG_J + 2] = hi, mid, lo
    tab[:, AUG_R] = 1.0
    return jnp.asarray(tab).astype(BF16)


def _diff_attn_kernel(slopes_ref, lamv_ref, q_ref, k_ref, v_ref, ktab_ref, gsub_ref, o_ref,
                      kcat, vcat, kstat, acc, m_scr, qpos, own_bias, *, tq, tk, seq, lam_init):
    h = pl.program_id(1)
    i = pl.program_id(2)
    n_chunks = seq // tk
    slope = slopes_ref[h]
    cs = slope * LOG2E
    nt = (((1,), (1,)), ((), ()))

    @pl.when(i == 0)
    def _():
        lane_k = lax.broadcasted_iota(jnp.int32, (tk, LANES), 1)
        ones_col = jnp.where(lane_k == 0, 1.0, 0.0).astype(BF16)

        def fill(c, carry):
            n1, n2 = carry
            sl = pl.ds(pl.multiple_of(c * tk, tk), tk)
            kc = k_ref[sl, :]
            kcat[sl, 0:LANES] = kc
            kcat[sl, LANES:2 * LANES] = ktab_ref[...]
            vcat[sl, 0:LANES] = v_ref[sl, :]
            vcat[sl, LANES:2 * LANES] = ones_col
            kf = kc.astype(F32)
            kk = kf * kf
            s1 = jnp.sum(jnp.where(lane_k < HD_A, kk, 0.0), axis=-1, keepdims=True)
            s2 = jnp.sum(jnp.where(lane_k >= HD_A, kk, 0.0), axis=-1, keepdims=True)
            return jnp.maximum(n1, s1), jnp.maximum(n2, s2)

        z = jnp.zeros((tk, 1), F32)
        n1, n2 = lax.fori_loop(0, n_chunks, fill, (z, z))
        kstat[0:1, :] = jnp.broadcast_to(jnp.sqrt(jnp.max(n1, axis=0, keepdims=True)), (1, LANES))
        kstat[1:2, :] = jnp.broadcast_to(jnp.sqrt(jnp.max(n2, axis=0, keepdims=True)), (1, LANES))

        row2 = lax.broadcasted_iota(jnp.int32, (2 * tq, LANES), 0)
        lane2 = lax.broadcasted_iota(jnp.int32, (2 * tq, LANES), 1)
        hi, mid, lo = _split3(cs * jnp.where(row2 >= tq, row2 - tq, row2).astype(F32))
        ppos = jnp.where(lane2 == AUG_P, hi, jnp.where(lane2 == AUG_P + 1, mid,
                                                       jnp.where(lane2 == AUG_P + 2, lo, 0.0)))
        qpos[...] = jnp.where((lane2 >= AUG_J) & (lane2 < AUG_J + 3), slope, 0.0) - ppos
        row = lax.broadcasted_iota(jnp.int32, (2 * tq, tq), 0)
        col = lax.broadcasted_iota(jnp.int32, (2 * tq, tq), 1)
        own_bias[...] = cs * jnp.abs(jnp.where(row >= tq, row - tq, row) - col).astype(F32)

    q = q_ref[...]
    lane = lax.broadcasted_iota(jnp.int32, (tq, LANES), 1)
    qf = q.astype(F32)
    qq = qf * qf
    qn1 = jnp.sqrt(jnp.sum(jnp.where(lane < HD_A, qq, 0.0), axis=-1, keepdims=True))
    qn2 = jnp.sqrt(jnp.sum(jnp.where(lane >= HD_A, qq, 0.0), axis=-1, keepdims=True))
    bound = jnp.concatenate([qn1 * kstat[0:1, 0:1], qn2 * kstat[1:2, 0:1]], axis=0) * 1.01
    bound_ok = jnp.max(bound) <= BOUND_LIMIT

    zero = jnp.zeros_like(q)
    base = jnp.concatenate([jnp.where(lane < HD_A, q, zero), jnp.where(lane >= HD_A, q, zero)], axis=0)
    lane2 = lax.broadcasted_iota(jnp.int32, (2 * tq, LANES), 1)
    rcol = jnp.where(lane2 == AUG_R, REF_HEADROOM - bound, 0.0)
    q_left = jnp.concatenate([base, (rcol + qpos[...]).astype(BF16)], axis=1)
    q_right = jnp.concatenate([base, (rcol - qpos[...]).astype(BF16)], axis=1)
    q_diag = jnp.concatenate([base, rcol.astype(BF16)], axis=1)
    q_plain = jnp.concatenate([base, jnp.zeros_like(base)], axis=1)
    i0 = i * tq
    cd = i0 // tk

    def chunk(c):
        return pl.ds(pl.multiple_of(c * tk, tk), tk)

    def fixed_reference():
        acc[...] = jnp.zeros(acc.shape, F32)

        def folded(keys, left, base):
            off = cs * (i0 - base).astype(F32)
            s = lax.dot_general(jnp.where(left, q_left, q_right), kcat[keys, :], nt,
                                preferred_element_type=F32) + jnp.where(left, -off, off)
            acc[...] += jnp.dot(jnp.exp2(s).astype(BF16), vcat[keys, :], preferred_element_type=F32)

        def body(c, carry):
            cc = c + (c >= cd).astype(jnp.int32)
            folded(chunk(cc), cc < cd, cc * tk)
            return carry

        n_main = n_chunks - 1
        lax.fori_loop(0, n_main, body, 0, unroll=next(u for u in (15, 5, 3, 2, 1) if n_main % u == 0))

        own = (i0 - cd * tk) // tq
        s = lax.dot_general(q_diag, kcat[pl.ds(pl.multiple_of(i0, tq), tq), :], nt, preferred_element_type=F32)
        s = s - own_bias[...]
        acc[...] += jnp.dot(jnp.exp2(s).astype(BF16), vcat[pl.ds(pl.multiple_of(i0, tq), tq), :],
                            preferred_element_type=F32)
        for k in range(1, tk // tq):
            blk = (own + k) % (tk // tq)
            keys = pl.ds(pl.multiple_of(cd * tk + blk * tq, tq), tq)
            folded(keys, blk < own, cd * tk)

    def online_reference():
        acc[...] = jnp.zeros(acc.shape, F32)
        m_scr[...] = jnp.full(m_scr.shape, NEG_INF, F32)
        row = lax.broadcasted_iota(jnp.int32, (2 * tq, tq), 0)
        col = lax.broadcasted_iota(jnp.int32, (2 * tq, tq), 1)
        rel_own = (jnp.where(row >= tq, row - tq, row) - col).astype(F32)

        def body(c, carry):
            keys = pl.ds(pl.multiple_of(c * tq, tq), tq)
            s = lax.dot_general(q_plain, kcat[keys, :], nt, preferred_element_type=F32)
            s = s - cs * jnp.abs(rel_own + (i0 - c * tq).astype(F32))
            m_prev = m_scr[...]
            m_new = jnp.maximum(m_prev, jnp.max(s, axis=-1, keepdims=True))
            p = jnp.exp2(s - m_new)
            acc[...] = jnp.exp2(m_prev - m_new) * acc[...] + jnp.dot(
                p.astype(BF16), vcat[keys, :], preferred_element_type=F32)
            m_scr[...] = m_new
            return carry

        lax.fori_loop(0, seq // tq, body, 0)

    lax.cond(bound_ok, fixed_reference, online_reference)

    lamv = lamv_ref[...]
    lam = (jnp.exp(jnp.sum(lamv[0:1] * lamv[1:2], axis=-1, keepdims=True))
           - jnp.exp(jnp.sum(lamv[2:3] * lamv[3:4], axis=-1, keepdims=True)) + lam_init)
    a = acc[...]
    o = a[:, 0:LANES] / a[:, LANES:LANES + 1]
    d = o[:tq] - lam * o[tq:]
    ms = jnp.mean(d * d, axis=-1, keepdims=True)
    o_ref[...] = (d * lax.rsqrt(ms + SUBLN_EPS) * gsub_ref[...] * (1.0 - lam_init)).astype(BF16)


def _diff_attn(z3, slopes_a, lamv, gsub, lam_init, tq, tk):
    B, S, _ = z3.shape
    assert tk % tq == 0 and S % tk == 0
    qb0, kb0, vb0 = COL_QA // LANES, COL_KA // LANES, COL_VA // LANES
    kernel = functools.partial(_diff_attn_kernel, tq=tq, tk=tk, seq=S, lam_init=lam_init)
    return pl.pallas_call(
        kernel,
        grid=(B, H_A, S // tq),
        in_specs=[
            pl.BlockSpec(memory_space=pltpu.SMEM),
            pl.BlockSpec((4, HD_A), lambda b, h, i: (0, 0)),
            pl.BlockSpec((None, tq, LANES), lambda b, h, i: (b, i, qb0 + h)),
            pl.BlockSpec((None, S, LANES), lambda b, h, i: (b, 0, kb0 + h)),
            pl.BlockSpec((None, S, LANES), lambda b, h, i: (b, 0, vb0 + h)),
            pl.BlockSpec((tk, LANES), lambda b, h, i: (0, 0)),
            pl.BlockSpec((1, LANES), lambda b, h, i: (0, 0)),
        ],
        out_specs=pl.BlockSpec((None, tq, LANES), lambda b, h, i: (b, i, h)),
        out_shape=jax.ShapeDtypeStruct((B, S, OA_W), BF16),
        scratch_shapes=[pltpu.VMEM((S, 2 * LANES), BF16), pltpu.VMEM((S, 2 * LANES), BF16),
                        pltpu.VMEM((8, LANES), F32), pltpu.VMEM((2 * tq, 2 * LANES), F32),
                        pltpu.VMEM((2 * tq, 1), F32), pltpu.VMEM((2 * tq, LANES), F32),
                        pltpu.VMEM((2 * tq, tq), F32)],
        compiler_params=pltpu.CompilerParams(
            dimension_semantics=("arbitrary", "arbitrary", "arbitrary"), vmem_limit_bytes=VMEM_LIMIT),
        name="diff_attn",
    )(slopes_a, lamv, z3, z3, z3, _key_table(tk), gsub)


BAND_STEP_BYTES = 2 * 1024 * 1024


def _band_attn_kernel(q_ref, kp_ref, kc_ref, kn_ref, vp_ref, vc_ref, vn_ref, o_ref, lse_ref,
                      *, length, dil, half, slopes, nsub):
    i = pl.program_id(1)
    qb = Q_BLOCK
    kw = qb + 2 * half
    nrow = H_B * qb
    row = lax.broadcasted_iota(jnp.int32, (nrow, kw), 0)
    col = lax.broadcasted_iota(jnp.int32, (nrow, kw), 1)
    rel = col - half - (row & (qb - 1))
    in_band = jnp.abs(rel) <= half
    slope_rows = jnp.full((nrow, kw), float(slopes[0]), F32)
    for hh in range(1, H_B):
        slope_rows = jnp.where(row >= hh * qb, float(slopes[hh]), slope_rows)
    bias = slope_rows * (dil * jnp.abs(rel)).astype(F32)
    valid = []
    for sb in range(nsub):
        kpos = (i * nsub + sb) * qb - half + col
        valid.append(in_band & (kpos >= 0) & (kpos < length))
    lane_head = lax.broadcasted_iota(jnp.int32, (qb, TN), 1) // HD_B
    nt = (((1,), (1,)), ((), ()))

    def window(prev_ref, cur_ref, next_ref, c, sb):
        lo, hi = sb * qb - half, sb * qb + qb + half
        parts = []
        if lo < 0:
            parts.append(prev_ref[c, qb + lo:qb, :])
        parts.append(cur_ref[c, max(lo, 0):min(hi, nsub * qb), :])
        if hi > nsub * qb:
            parts.append(next_ref[c, 0:hi - nsub * qb, :])
        return parts[0] if len(parts) == 1 else jnp.concatenate(parts, axis=0)

    def one_class(c, carry):
        for sb in range(nsub):
            start = c + sb * qb * dil
            rows = pl.ds(start, qb, stride=dil) if dil > 1 else pl.ds(sb * qb, qb)
            q4 = q_ref[c, sb * qb:(sb + 1) * qb, :]
            zero = jnp.zeros_like(q4)
            qbd = jnp.concatenate([jnp.where(lane_head == hh, q4, zero) for hh in range(H_B)], axis=0)
            kk = window(kp_ref, kc_ref, kn_ref, c, sb)
            vv = window(vp_ref, vc_ref, vn_ref, c, sb)
            s = lax.dot_general(qbd, kk, nt, preferred_element_type=F32)
            s = jnp.where(valid[sb], s - bias, NEG_INF)
            m = jnp.max(s, axis=-1, keepdims=True)
            p = jnp.exp(s - m)
            l = jnp.sum(p, axis=-1, keepdims=True)
            o_all = jnp.dot(p.astype(BF16), vv, preferred_element_type=F32)
            lse = m + jnp.log(l)
            for hh in range(H_B):
                rr = slice(hh * qb, (hh + 1) * qb)
                o_ref[hh, rows, :] = o_all[rr, hh * HD_B:(hh + 1) * HD_B] / l[rr]
                lse_ref[hh, rows, :] = jnp.broadcast_to(lse[rr], (qb, HD_B))
        return carry

    lax.fori_loop(0, dil, one_class, 0, unroll=min(dil, max(1, 4 // nsub)))


def _band_attn(qkv, g, window, dil):
    _, B, _, L, _ = qkv.shape
    S = L * dil
    half = (window // 2) // dil
    assert half <= Q_BLOCK and half % 16 == 0
    nb = L // Q_BLOCK
    nsub = max(1, min(4, nb, BAND_STEP_BYTES // (dil * Q_BLOCK * TN * 2)))
    assert nb % nsub == 0
    slopes = _alibi_slopes(N_GROUPS_B * H_B).reshape(N_GROUPS_B, H_B)[g]
    kernel = functools.partial(_band_attn_kernel, length=L, dil=dil, half=half, slopes=slopes, nsub=nsub)

    def own(sec):
        return pl.BlockSpec((None, None, dil, nsub * Q_BLOCK, TN), lambda b, i: (sec, b, 0, i, 0))

    def neighbour(sec, after):
        def imap(b, i):
            blk = (i + 1) * nsub if after else i * nsub - 1
            return (sec, b, 0, jnp.clip(blk, 0, nb - 1), 0)
        return pl.BlockSpec((None, None, dil, Q_BLOCK, TN), imap)

    out_spec = pl.BlockSpec((None, H_B, nsub * Q_BLOCK * dil, HD_B), lambda b, i: (b, 0, i, 0))
    out_shape = jax.ShapeDtypeStruct((B, H_B, S, HD_B), F32)
    return pl.pallas_call(
        kernel,
        grid=(B, nb // nsub),
        in_specs=[own(0), neighbour(1, False), own(1), neighbour(1, True),
                  neighbour(2, False), own(2), neighbour(2, True)],
        out_specs=[out_spec, out_spec],
        out_shape=[out_shape, out_shape],
        compiler_params=pltpu.CompilerParams(
            dimension_semantics=("arbitrary", "arbitrary"), vmem_limit_bytes=VMEM_LIMIT),
        name=f"band_attn_g{g}",
    )(qkv, qkv, qkv, qkv, qkv, qkv, qkv)


def _merge_kernel(x_ref, ga_ref, gb_ref, oa_ref, o0_ref, o1_ref, o2_ref, l0_ref, l1_ref, l2_ref,
                  wpa_ref, wpb_ref, wo_ref, gffn_ref, x1_ref, h2_ref):
    heads = []
    for hh in range(H_B):
        l0, l1, l2 = l0_ref[hh], l1_ref[hh], l2_ref[hh]
        m = jnp.maximum(jnp.maximum(l0, l1), l2)
        w0, w1, w2 = jnp.exp(l0 - m), jnp.exp(l1 - m), jnp.exp(l2 - m)
        heads.append(((w0 * o0_ref[hh] + w1 * o1_ref[hh] + w2 * o2_ref[hh]) / (w0 + w1 + w2)).astype(BF16))
    ob = jnp.concatenate(heads, axis=1)
    pa = jnp.dot(oa_ref[...], wpa_ref[...], preferred_element_type=F32)
    pb = jnp.dot(ob, wpb_ref[...], preferred_element_type=F32)
    sigmoid = lambda g: 0.5 * jnp.tanh(0.5 * g.astype(F32)) + 0.5
    merged = sigmoid(ga_ref[...]) * pa + sigmoid(gb_ref[...]) * pb
    x1 = x_ref[...] + jnp.dot(merged.astype(BF16), wo_ref[...], preferred_element_type=F32)
    x1_ref[...] = x1
    ms = jnp.mean(x1 * x1, axis=-1, keepdims=True)
    h2_ref[...] = (x1 * lax.rsqrt(ms + NORM_EPS) * gffn_ref[...]).astype(BF16)


def _merge(x2, z2, oa2, obs, lses, wpa, wpb, wo, gffn, seq, tm):
    T = x2.shape[0]
    tiles_per_seq = seq // tm
    row = lambda w: pl.BlockSpec((tm, w), lambda i: (i, 0))
    per_head = pl.BlockSpec((None, H_B, tm, HD_B), lambda i: (i // tiles_per_seq, 0, i % tiles_per_seq, 0))
    const = lambda a: pl.BlockSpec(a.shape, lambda i: (0, 0), pipeline_mode=pl.Buffered(1))
    return pl.pallas_call(
        _merge_kernel,
        grid=(T // tm,),
        in_specs=[row(D_MODEL),
                  pl.BlockSpec((tm, D_MODEL), lambda i: (i, COL_GATE_A // D_MODEL)),
                  pl.BlockSpec((tm, D_MODEL), lambda i: (i, COL_GATE_B // D_MODEL)),
                  row(OA_W)] + [per_head] * 6 + [const(wpa), const(wpb), const(wo), const(gffn)],
        out_specs=[row(D_MODEL), row(D_MODEL)],
        out_shape=[jax.ShapeDtypeStruct((T, D_MODEL), F32), jax.ShapeDtypeStruct((T, D_MODEL), BF16)],
        compiler_params=pltpu.CompilerParams(
            dimension_semantics=("parallel",), vmem_limit_bytes=VMEM_LIMIT),
        name="merge",
    )(x2, z2, z2, oa2, *obs, *lses, wpa, wpb, wo, gffn)


HALO = 16


def _ffn_kernel(h_ref, hp_ref, hn_ref, x1_ref, wg_ref, wv_ref, cwg_ref, cwv_ref, cbg_ref, cbv_ref,
                wd_ref, o_ref, hcat_scr, acc_scr, *, tm, tiles_per_seq, n_f):
    i = pl.program_id(0)
    f = pl.program_id(1)

    @pl.when(f == 0)
    def _():
        pos = i % tiles_per_seq
        hp = hp_ref[...]
        hn = hn_ref[...]
        hcat_scr[0:HALO, :] = jnp.where(pos == 0, jnp.zeros_like(hp), hp)
        hcat_scr[HALO:HALO + tm, :] = h_ref[...]
        hcat_scr[HALO + tm:, :] = jnp.where(pos == tiles_per_seq - 1, jnp.zeros_like(hn), hn)
        acc_scr[...] = jnp.zeros(acc_scr.shape, F32)

    hcat = hcat_scr[...]
    rows = tm + 2 * HALO

    def conv_half(w_ref, cw_ref, cb_ref):
        a = jnp.dot(hcat, w_ref[...], preferred_element_type=F32)
        a_prev = pltpu.roll(a, 1, 0)[HALO:HALO + tm]
        a_next = pltpu.roll(a, rows - 1, 0)[HALO:HALO + tm]
        cw = cw_ref[...]
        return a_prev * cw[0:1] + a[HALO:HALO + tm] * cw[1:2] + a_next * cw[2:3] + cb_ref[...]

    ug = conv_half(wg_ref, cwg_ref, cbg_ref)
    uv = conv_half(wv_ref, cwv_ref, cbv_ref)
    act = 0.5 * ug * (1.0 + lax.erf(ug * np.float32(math.sqrt(0.5)))) * uv
    acc_scr[...] += jnp.dot(act.astype(BF16), wd_ref[...], preferred_element_type=F32)

    @pl.when(f == n_f - 1)
    def _():
        o_ref[...] = x1_ref[...] + acc_scr[...]


def _ffn(h2, x1, w_up, conv_w, conv_b, w_down, seq, tm, tf):
    T = h2.shape[0]
    n_f = D_FF // tf
    hb = tm // HALO
    n_hb = T // HALO
    kernel = functools.partial(_ffn_kernel, tm=tm, tiles_per_seq=seq // tm, n_f=n_f)
    return pl.pallas_call(
        kernel,
        grid=(T // tm, n_f),
        in_specs=[
            pl.BlockSpec((tm, D_MODEL), lambda i, f: (i, 0)),
            pl.BlockSpec((HALO, D_MODEL), lambda i, f: (jnp.maximum(i * hb - 1, 0), 0)),
            pl.BlockSpec((HALO, D_MODEL), lambda i, f: (jnp.minimum((i + 1) * hb, n_hb - 1), 0)),
            pl.BlockSpec((tm, D_MODEL), lambda i, f: (i, 0)),
            pl.BlockSpec((D_MODEL, tf), lambda i, f: (0, f)),
            pl.BlockSpec((D_MODEL, tf), lambda i, f: (0, f + n_f)),
            pl.BlockSpec((3, tf), lambda i, f: (0, f)),
            pl.BlockSpec((3, tf), lambda i, f: (0, f + n_f)),
            pl.BlockSpec((1, tf), lambda i, f: (0, f)),
            pl.BlockSpec((1, tf), lambda i, f: (0, f + n_f)),
            pl.BlockSpec((tf, D_MODEL), lambda i, f: (f, 0)),
        ],
        out_specs=pl.BlockSpec((tm, D_MODEL), lambda i, f: (i, 0)),
        out_shape=jax.ShapeDtypeStruct((T, D_MODEL), F32),
        scratch_shapes=[pltpu.VMEM((tm + 2 * HALO, D_MODEL), BF16), pltpu.VMEM((tm, D_MODEL), F32)],
        compiler_params=pltpu.CompilerParams(
            dimension_semantics=("parallel", "arbitrary"), vmem_limit_bytes=VMEM_LIMIT),
        name="ffn",
    )(h2, h2, h2, x1, w_up, w_up, conv_w, conv_w, conv_b, conv_b, w_down)


def _prep_params(g_mix_norm, w_in, g_qa, g_ka, lam_q1, lam_k1, lam_q2, lam_k2, g_subln, g_qb, g_kb,
                 w_pa, w_pb, w_o, g_ffn_norm, w_up, conv_w, conv_b, w_down):
    qa, ka, va, qb, kb, vb, gate_a, gate_b = jnp.split(
        w_in, list(np.cumsum([QA_W, QA_W, QA_W, QB_W, QB_W, QB_W, D_MODEL])), axis=-1)
    w_in_p = jnp.concatenate([gate_a, gate_b, qa, ka, va, qb, kb, vb], axis=-1).astype(BF16)
    ones = lambda n: jnp.ones((n,), F32)
    gvec = jnp.concatenate([
        ones(2 * D_MODEL),
        jnp.tile(g_qa.astype(F32), 2 * H_A) * (HD_A ** -0.5 * LOG2E),
        jnp.tile(g_ka.astype(F32), 2 * H_A),
        ones(QA_W),
        jnp.tile(g_qb.astype(F32), N_GROUPS_B * H_B) * (HD_B ** -0.5),
        jnp.tile(g_kb.astype(F32), N_GROUPS_B * H_B),
        ones(QB_W)]).reshape(1, N_IN)
    idx = np.arange(TN)
    gmat = jnp.asarray(np.stack([
        (idx[:, None] // HD_A == idx[None, :] // HD_A) / HD_A,
        (idx[:, None] // HD_B == idx[None, :] // HD_B) / HD_B]).astype(np.float32), BF16)
    lamv = jnp.stack([lam_q1, lam_k1, lam_q2, lam_k2]).astype(F32)
    return dict(
        gmix=g_mix_norm.astype(F32).reshape(1, D_MODEL), w_in=w_in_p, gvec=gvec, gmat=gmat, lamv=lamv,
        gsub=g_subln.astype(F32).reshape(1, 2 * HD_A), wpa=w_pa.astype(BF16), wpb=w_pb.astype(BF16),
        wo=w_o.astype(BF16), gffn=g_ffn_norm.astype(F32).reshape(1, D_MODEL), w_up=w_up.astype(BF16),
        conv_w=conv_w.astype(F32), conv_b=conv_b.astype(F32).reshape(1, 2 * D_FF),
        w_down=w_down.astype(BF16))


def _tiles(seq):
    return dict(in_proj_tm=min(1024, seq), attn_tq=min(512, seq // 2), attn_tk=min(1024, seq // 2),
                merge_tm=256, ffn_tm=512, ffn_tf=512)


def _encoder_layer(x, layer_idx, p):
    B, S, _ = x.shape
    T = B * S
    t = _tiles(S)
    lam_init = 0.8 - 0.6 * math.exp(-0.3 * layer_idx)
    x2 = x.reshape(T, D_MODEL)
    z2, *groups = _in_proj(x2, p["gmix"], p["w_in"], p["gvec"], p["gmat"], batch=B, seq=S,
                           tm=t["in_proj_tm"])
    z3 = z2.reshape(B, S, COL_QB)
    oa = _diff_attn(z3, jnp.asarray(_alibi_slopes(H_A)), p["lamv"], p["gsub"], lam_init,
                    tq=t["attn_tq"], tk=t["attn_tk"])
    obs, lses = [], []
    for g, (window, dil) in enumerate(DIL_CONFIGS):
        o, lse = _band_attn(groups[g], g, window, dil)
        obs.append(o)
        lses.append(lse)
    x1, h2 = _merge(x2, z2, oa.reshape(T, OA_W), obs, lses, p["wpa"], p["wpb"], p["wo"], p["gffn"],
                    seq=S, tm=t["merge_tm"])
    y = _ffn(h2, x1, p["w_up"], p["conv_w"], p["conv_b"], p["w_down"], seq=S, tm=t["ffn_tm"], tf=t["ffn_tf"])
    return y.reshape(B, S, D_MODEL)


def kernel(x_prompt, x_sample, g_mix_norm, w_in, g_qa, g_ka, lam_q1, lam_k1, lam_q2, lam_k2, g_subln,
           g_qb, g_kb, w_pa, w_pb, w_o, g_ffn_norm, w_up, conv_w, conv_b, w_down):
    y_prompt, y_sample = x_prompt, x_sample
    for l in range(w_in.shape[0]):
        p = _prep_params(g_mix_norm[l], w_in[l], g_qa[l], g_ka[l], lam_q1[l], lam_k1[l], lam_q2[l],
                         lam_k2[l], g_subln[l], g_qb[l], g_kb[l], w_pa[l], w_pb[l], w_o[l],
                         g_ffn_norm[l], w_up[l], conv_w[l], conv_b[l], w_down[l])
        y_prompt = _encoder_layer(y_prompt, l, p)
        y_sample = _encoder_layer(y_sample, l, p)
    return (y_prompt, y_sample)
```

```python
import functools
import math

import numpy as np
import jax
import jax.numpy as jnp
from jax import lax
from jax.experimental import pallas as pl
from jax.experimental.pallas import tpu as pltpu

D_MODEL = 2048
H_A = 8
HD_A = 64
DIL_CONFIGS = ((128, 1), (512, 4), (2048, 16))
N_GROUPS_B = 3
H_B = 4
HD_B = 128
D_FF = 5632
Q_BLOCK = 128
NORM_EPS = 1e-6
SUBLN_EPS = 1e-5
NEG_INF = -1e30

QA_W = H_A * 2 * HD_A
QB_W = N_GROUPS_B * H_B * HD_B
OA_W = QA_W
OB_W = H_B * HD_B
N_IN = 3 * QA_W + 3 * QB_W + 2 * D_MODEL

LANES = 128
COL_GATE_A = 0
COL_GATE_B = D_MODEL
COL_QA = 2 * D_MODEL
COL_KA = COL_QA + QA_W
COL_VA = COL_KA + QA_W
COL_QB = COL_VA + QA_W
COL_KB = COL_QB + QB_W
COL_VB = COL_KB + QB_W
TN = 512
N_TILES = N_IN // TN
VMEM_LIMIT = 56 * 1024 * 1024

F32 = jnp.float32
BF16 = jnp.bfloat16


def _alibi_slopes(n):
    return np.asarray(2.0 ** (-8.0 * (np.arange(n) + 1) / n), dtype=np.float32)


T_QA, T_VA, T_QB, T_KB, T_VB = COL_QA // TN, COL_VA // TN, COL_QB // TN, COL_KB // TN, COL_VB // TN
DILS = tuple(d for _, d in DIL_CONFIGS)


def _in_proj_kernel(x_ref, gmix_ref, w_ref, gvec_ref, gmat_ref, z_ref, g0_ref, g1_ref, g2_ref,
                    h_scr, t_scr, *, tm):
    j = pl.program_id(1)

    @pl.when(j == 0)
    def _():
        x = x_ref[...]
        ms = jnp.mean(x * x, axis=-1, keepdims=True)
        h_scr[...] = (x * lax.rsqrt(ms + NORM_EPS) * gmix_ref[...]).astype(BF16)

    def plain():
        return jnp.dot(h_scr[...], w_ref[...], preferred_element_type=F32)

    def normed():
        z = plain()
        ms = jnp.dot((z * z).astype(BF16), gmat_ref[...], preferred_element_type=F32)
        return z * lax.rsqrt(ms + NORM_EPS) * gvec_ref[...]

    @pl.when((j >= T_QA) & (j < T_VA))
    def _():
        z_ref[...] = normed().astype(BF16)

    @pl.when((j < T_QA) | ((j >= T_VA) & (j < T_QB)))
    def _():
        z_ref[...] = plain().astype(BF16)

    def scatter(g_ref, dil, val):
        if dil == 1:
            g_ref[0] = val.astype(BF16)
        else:
            for kb in range(TN // LANES):
                t_scr[kb] = val[:, kb * LANES:(kb + 1) * LANES]
            for c in range(dil):
                for kb in range(TN // LANES):
                    g_ref[c, :, kb * LANES:(kb + 1) * LANES] = (
                        t_scr[kb, pl.ds(c, tm // dil, stride=dil), :].astype(BF16))

    for g, (g_ref, dil) in enumerate(zip((g0_ref, g1_ref, g2_ref), DILS)):
        @pl.when((j == T_QB + g) | (j == T_KB + g))
        def _(g_ref=g_ref, dil=dil):
            scatter(g_ref, dil, normed())

        @pl.when(j == T_VB + g)
        def _(g_ref=g_ref, dil=dil):
            scatter(g_ref, dil, plain())


def _in_proj(x2, gmix, w_in_p, gvec, gmat, batch, seq, tm):
    T = x2.shape[0]
    tiles_per_seq = seq // tm
    main_w = COL_QB

    def group_spec(g, dil):
        def imap(i, j):
            sec = (j >= T_KB + g).astype(jnp.int32) + (j >= T_VB + g).astype(jnp.int32)
            return (sec, i // tiles_per_seq, 0, i % tiles_per_seq, 0)
        return pl.BlockSpec((None, None, dil, tm // dil, TN), imap)

    return pl.pallas_call(
        functools.partial(_in_proj_kernel, tm=tm),
        grid=(T // tm, N_TILES),
        in_specs=[
            pl.BlockSpec((tm, D_MODEL), lambda i, j: (i, 0)),
            pl.BlockSpec((1, D_MODEL), lambda i, j: (0, 0)),
            pl.BlockSpec((D_MODEL, TN), lambda i, j: (0, j)),
            pl.BlockSpec((1, TN), lambda i, j: (0, j)),
            pl.BlockSpec((None, TN, TN), lambda i, j: (jnp.where(j < T_QB, 0, 1), 0, 0)),
        ],
        out_specs=[pl.BlockSpec((tm, TN), lambda i, j: (i, jnp.minimum(j, T_QB - 1)))]
                  + [group_spec(g, dil) for g, dil in enumerate(DILS)],
        out_shape=[jax.ShapeDtypeStruct((T, main_w), BF16)]
                  + [jax.ShapeDtypeStruct((3, batch, dil, seq // dil, TN), BF16) for dil in DILS],
        scratch_shapes=[pltpu.VMEM((tm, D_MODEL), BF16), pltpu.VMEM((TN // LANES, tm, LANES), F32)],
        compiler_params=pltpu.CompilerParams(
            dimension_semantics=("arbitrary", "arbitrary"), vmem_limit_bytes=VMEM_LIMIT),
        name="in_proj",
    )(x2, gmix, w_in_p, gvec, gmat)


LOG2E = math.log2(math.e)
AUG_P, AUG_J, AUG_R = 0, 3, 6
REF_HEADROOM = 60.0
BOUND_LIMIT = 80.0


def _split3(x):
    def top(a):
        bits = lax.bitcast_convert_type(a, jnp.uint32) & jnp.uint32(0xFFFF0000)
        return lax.bitcast_convert_type(bits, F32)
    hi = top(x)
    mid = top(x - hi)
    lo = top(x - hi - mid)
    return hi, mid, lo


def _key_table(tk):
    def top(a):
        return (a.view(np.uint32) & np.uint32(0xFFFF0000)).view(np.float32)
    x = (np.arange(tk, dtype=np.float32) * np.float32(LOG2E)).astype(np.float32)
    hi = top(x)
    mid = top(x - hi)
    lo = top(x - hi - mid)
    tab = np.zeros((tk, LANES), np.float32)
    tab[:, AUG_P:AUG_P + 3] = 1.0
    tab[:, AUG_J], tab[:, AUG_J + 1], tab[:, AUG_J + 2] = hi, mid, lo
    tab[:, AUG_R] = 1.0
    return jnp.asarray(tab).astype(BF16)


def _diff_attn_kernel(slopes_ref, lamv_ref, q_ref, k_ref, v_ref, ktab_ref, gsub_ref, o_ref,
                      kcat, vcat, kstat, acc, m_scr, qpos, own_bias, *, tq, tk, seq, lam_init):
    h = pl.program_id(1)
    i = pl.program_id(2)
    n_chunks = seq // tk
    slope = slopes_ref[h]
    cs = slope * LOG2E
    nt = (((1,), (1,)), ((), ()))

    @pl.when(i == 0)
    def _():
        lane_k = lax.broadcasted_iota(jnp.int32, (tk, LANES), 1)
        ones_col = jnp.where(lane_k == 0, 1.0, 0.0).astype(BF16)

        def fill(c, carry):
            n1, n2 = carry
            sl = pl.ds(pl.multiple_of(c * tk, tk), tk)
            kc = k_ref[sl, :]
            kcat[sl, 0:LANES] = kc
            kcat[sl, LANES:2 * LANES] = ktab_ref[...]
            vcat[sl, 0:LANES] = v_ref[sl, :]
            vcat[sl, LANES:2 * LANES] = ones_col
            kf = kc.astype(F32)
            kk = kf * kf
            s1 = jnp.sum(jnp.where(lane_k < HD_A, kk, 0.0), axis=-1, keepdims=True)
            s2 = jnp.sum(jnp.where(lane_k >= HD_A, kk, 0.0), axis=-1, keepdims=True)
            return jnp.maximum(n1, s1), jnp.maximum(n2, s2)

        z = jnp.zeros((tk, 1), F32)
        n1, n2 = lax.fori_loop(0, n_chunks, fill, (z, z))
        kstat[0:1, :] = jnp.broadcast_to(jnp.sqrt(jnp.max(n1, axis=0, keepdims=True)), (1, LANES))
        kstat[1:2, :] = jnp.broadcast_to(jnp.sqrt(jnp.max(n2, axis=0, keepdims=True)), (1, LANES))

        row2 = lax.broadcasted_iota(jnp.int32, (2 * tq, LANES), 0)
        lane2 = lax.broadcasted_iota(jnp.int32, (2 * tq, LANES), 1)
        hi, mid, lo = _split3(cs * jnp.where(row2 >= tq, row2 - tq, row2).astype(F32))
        ppos = jnp.where(lane2 == AUG_P, hi, jnp.where(lane2 == AUG_P + 1, mid,
                                                       jnp.where(lane2 == AUG_P + 2, lo, 0.0)))
        qpos[...] = jnp.where((lane2 >= AUG_J) & (lane2 < AUG_J + 3), slope, 0.0) - ppos
        row = lax.broadcasted_iota(jnp.int32, (2 * tq, tq), 0)
        col = lax.broadcasted_iota(jnp.int32, (2 * tq, tq), 1)
        own_bias[...] = cs * jnp.abs(jnp.where(row >= tq, row - tq, row) - col).astype(F32)

    q = q_ref[...]
    lane = lax.broadcasted_iota(jnp.int32, (tq, LANES), 1)
    qf = q.astype(F32)
    qq = qf * qf
    qn1 = jnp.sqrt(jnp.sum(jnp.where(lane < HD_A, qq, 0.0), axis=-1, keepdims=True))
    qn2 = jnp.sqrt(jnp.sum(jnp.where(lane >= HD_A, qq, 0.0), axis=-1, keepdims=True))
    bound = jnp.concatenate([qn1 * kstat[0:1, 0:1], qn2 * kstat[1:2, 0:1]], axis=0) * 1.01
    bound_ok = jnp.max(bound) <= BOUND_LIMIT

    zero = jnp.zeros_like(q)
    base = jnp.concatenate([jnp.where(lane < HD_A, q, zero), jnp.where(lane >= HD_A, q, zero)], axis=0)
    lane2 = lax.broadcasted_iota(jnp.int32, (2 * tq, LANES), 1)
    rcol = jnp.where(lane2 == AUG_R, REF_HEADROOM - bound, 0.0)
    q_left = jnp.concatenate([base, (rcol + qpos[...]).astype(BF16)], axis=1)
    q_right = jnp.concatenate([base, (rcol - qpos[...]).astype(BF16)], axis=1)
    q_diag = jnp.concatenate([base, rcol.astype(BF16)], axis=1)
    q_plain = jnp.concatenate([base, jnp.zeros_like(base)], axis=1)
    i0 = i * tq
    cd = i0 // tk

    def chunk(c):
        return pl.ds(pl.multiple_of(c * tk, tk), tk)

    def fixed_reference():
        acc[...] = jnp.zeros(acc.shape, F32)

        def folded(keys, left, base):
            off = cs * (i0 - base).astype(F32)
            s = lax.dot_general(jnp.where(left, q_left, q_right), kcat[keys, :], nt,
                                preferred_element_type=F32) + jnp.where(left, -off, off)
            acc[...] += jnp.dot(jnp.exp2(s).astype(BF16), vcat[keys, :], preferred_element_type=F32)

        def body(c, carry):
            cc = c + (c >= cd).astype(jnp.int32)
            folded(chunk(cc), cc < cd, cc * tk)
            return carry

        n_main = n_chunks - 1
        lax.fori_loop(0, n_main, body, 0, unroll=next(u for u in (15, 5, 3, 2, 1) if n_main % u == 0))

        own = (i0 - cd * tk) // tq
        s = lax.dot_general(q_diag, kcat[pl.ds(pl.multiple_of(i0, tq), tq), :], nt, preferred_element_type=F32)
        s = s - own_bias[...]
        acc[...] += jnp.dot(jnp.exp2(s).astype(BF16), vcat[pl.ds(pl.multiple_of(i0, tq), tq), :],
                            preferred_element_type=F32)
        for k in range(1, tk // tq):
            blk = (own + k) % (tk // tq)
            keys = pl.ds(pl.multiple_of(cd * tk + blk * tq, tq), tq)
            folded(keys, blk < own, cd * tk)

    def online_reference():
        acc[...] = jnp.zeros(acc.shape, F32)
        m_scr[...] = jnp.full(m_scr.shape, NEG_INF, F32)
        row = lax.broadcasted_iota(jnp.int32, (2 * tq, tq), 0)
        col = lax.broadcasted_iota(jnp.int32, (2 * tq, tq), 1)
        rel_own = (jnp.where(row >= tq, row - tq, row) - col).astype(F32)

        def body(c, carry):
            keys = pl.ds(pl.multiple_of(c * tq, tq), tq)
            s = lax.dot_general(q_plain, kcat[keys, :], nt, preferred_element_type=F32)
            s = s - cs * jnp.abs(rel_own + (i0 - c * tq).astype(F32))
            m_prev = m_scr[...]
            m_new = jnp.maximum(m_prev, jnp.max(s, axis=-1, keepdims=True))
            p = jnp.exp2(s - m_new)
            acc[...] = jnp.exp2(m_prev - m_new) * acc[...] + jnp.dot(
                p.astype(BF16), vcat[keys, :], preferred_element_type=F32)
            m_scr[...] = m_new
            return carry

        lax.fori_loop(0, seq // tq, body, 0)

    lax.cond(bound_ok, fixed_reference, online_reference)

    lamv = lamv_ref[...]
    lam = (jnp.exp(jnp.sum(lamv[0:1] * lamv[1:2], axis=-1, keepdims=True))
           - jnp.exp(jnp.sum(lamv[2:3] * lamv[3:4], axis=-1, keepdims=True)) + lam_init)
    a = acc[...]
    o = a[:, 0:LANES] / a[:, LANES:LANES + 1]
    d = o[:tq] - lam * o[tq:]
    ms = jnp.mean(d * d, axis=-1, keepdims=True)
    o_ref[...] = (d * lax.rsqrt(ms + SUBLN_EPS) * gsub_ref[...] * (1.0 - lam_init)).astype(BF16)


def _diff_attn(z3, slopes_a, lamv, gsub, lam_init, tq, tk):
    B, S, _ = z3.shape
    assert tk % tq == 0 and S % tk == 0
    qb0, kb0, vb0 = COL_QA // LANES, COL_KA // LANES, COL_VA // LANES
    kernel = functools.partial(_diff_attn_kernel, tq=tq, tk=tk, seq=S, lam_init=lam_init)
    return pl.pallas_call(
        kernel,
        grid=(B, H_A, S // tq),
        in_specs=[
            pl.BlockSpec(memory_space=pltpu.SMEM),
            pl.BlockSpec((4, HD_A), lambda b, h, i: (0, 0)),
            pl.BlockSpec((None, tq, LANES), lambda b, h, i: (b, i, qb0 + h)),
            pl.BlockSpec((None, S, LANES), lambda b, h, i: (b, 0, kb0 + h)),
            pl.BlockSpec((None, S, LANES), lambda b, h, i: (b, 0, vb0 + h)),
            pl.BlockSpec((tk, LANES), lambda b, h, i: (0, 0)),
            pl.BlockSpec((1, LANES), lambda b, h, i: (0, 0)),
        ],
        out_specs=pl.BlockSpec((None, tq, LANES), lambda b, h, i: (b, i, h)),
        out_shape=jax.ShapeDtypeStruct((B, S, OA_W), BF16),
        scratch_shapes=[pltpu.VMEM((S, 2 * LANES), BF16), pltpu.VMEM((S, 2 * LANES), BF16),
                        pltpu.VMEM((8, LANES), F32), pltpu.VMEM((2 * tq, 2 * LANES), F32),
                        pltpu.VMEM((2 * tq, 1), F32), pltpu.VMEM((2 * tq, LANES), F32),
                        pltpu.VMEM((2 * tq, tq), F32)],
        compiler_params=pltpu.CompilerParams(
            dimension_semantics=("arbitrary", "arbitrary", "arbitrary"), vmem_limit_bytes=VMEM_LIMIT),
        name="diff_attn",
    )(slopes_a, lamv, z3, z3, z3, _key_table(tk), gsub)


BAND_STEP_BYTES = 2 * 1024 * 1024


def _band_attn_kernel(q_ref, kp_ref, kc_ref, kn_ref, vp_ref, vc_ref, vn_ref, o_ref, lse_ref,
                      *, length, dil, half, slopes, nsub):
    i = pl.program_id(1)
    qb = Q_BLOCK
    kw = qb + 2 * half
    nrow = H_B * qb
    row = lax.broadcasted_iota(jnp.int32, (nrow, kw), 0)
    col = lax.broadcasted_iota(jnp.int32, (nrow, kw), 1)
    rel = col - half - (row & (qb - 1))
    in_band = jnp.abs(rel) <= half
    slope_rows = jnp.full((nrow, kw), float(slopes[0]), F32)
    for hh in range(1, H_B):
        slope_rows = jnp.where(row >= hh * qb, float(slopes[hh]), slope_rows)
    bias = slope_rows * (dil * jnp.abs(rel)).astype(F32)
    valid = []
    for sb in range(nsub):
        kpos = (i * nsub + sb) * qb - half + col
        valid.append(in_band & (kpos >= 0) & (kpos < length))
    lane_head = lax.broadcasted_iota(jnp.int32, (qb, TN), 1) // HD_B
    nt = (((1,), (1,)), ((), ()))

    def window(prev_ref, cur_ref, next_ref, c, sb):
        lo, hi = sb * qb - half, sb * qb + qb + half
        parts = []
        if lo < 0:
            parts.append(prev_ref[c, qb + lo:qb, :])
        parts.append(cur_ref[c, max(lo, 0):min(hi, nsub * qb), :])
        if hi > nsub * qb:
            parts.append(next_ref[c, 0:hi - nsub * qb, :])
        return parts[0] if len(parts) == 1 else jnp.concatenate(parts, axis=0)

    def one_class(c, carry):
        for sb in range(nsub):
            start = c + sb * qb * dil
            rows = pl.ds(start, qb, stride=dil) if dil > 1 else pl.ds(sb * qb, qb)
            q4 = q_ref[c, sb * qb:(sb + 1) * qb, :]
            zero = jnp.zeros_like(q4)
            qbd = jnp.concatenate([jnp.where(lane_head == hh, q4, zero) for hh in range(H_B)], axis=0)
            kk = window(kp_ref, kc_ref, kn_ref, c, sb)
            vv = window(vp_ref, vc_ref, vn_ref, c, sb)
            s = lax.dot_general(qbd, kk, nt, preferred_element_type=F32)
            s = jnp.where(valid[sb], s - bias, NEG_INF)
            m = jnp.max(s, axis=-1, keepdims=True)
            p = jnp.exp(s - m)
            l = jnp.sum(p, axis=-1, keepdims=True)
            o_all = jnp.dot(p.astype(BF16), vv, preferred_element_type=F32)
            lse = m + jnp.log(l)
            for hh in range(H_B):
                rr = slice(hh * qb, (hh + 1) * qb)
                o_ref[hh, rows, :] = o_all[rr, hh * HD_B:(hh + 1) * HD_B] / l[rr]
                lse_ref[hh, rows, :] = jnp.broadcast_to(lse[rr], (qb, HD_B))
        return carry

    lax.fori_loop(0, dil, one_class, 0, unroll=min(dil, max(1, 4 // nsub)))


def _band_attn(qkv, g, window, dil):
    _, B, _, L, _ = qkv.shape
    S = L * dil
    half = (window // 2) // dil
    assert half <= Q_BLOCK and half % 16 == 0
    nb = L // Q_BLOCK
    nsub = max(1, min(4, nb, BAND_STEP_BYTES // (dil * Q_BLOCK * TN * 2)))
    assert nb % nsub == 0
    slopes = _alibi_slopes(N_GROUPS_B * H_B).reshape(N_GROUPS_B, H_B)[g]
    kernel = functools.partial(_band_attn_kernel, length=L, dil=dil, half=half, slopes=slopes, nsub=nsub)

    def own(sec):
        return pl.BlockSpec((None, None, dil, nsub * Q_BLOCK, TN), lambda b, i: (sec, b, 0, i, 0))

    def neighbour(sec, after):
        def imap(b, i):
            blk = (i + 1) * nsub if after else i * nsub - 1
            return (sec, b, 0, jnp.clip(blk, 0, nb - 1), 0)
        return pl.BlockSpec((None, None, dil, Q_BLOCK, TN), imap)

    out_spec = pl.BlockSpec((None, H_B, nsub * Q_BLOCK * dil, HD_B), lambda b, i: (b, 0, i, 0))
    out_shape = jax.ShapeDtypeStruct((B, H_B, S, HD_B), F32)
    return pl.pallas_call(
        kernel,
        grid=(B, nb // nsub),
        in_specs=[own(0), neighbour(1, False), own(1), neighbour(1, True),
                  neighbour(2, False), own(2), neighbour(2, True)],
        out_specs=[out_spec, out_spec],
        out_shape=[out_shape, out_shape],
        compiler_params=pltpu.CompilerParams(
            dimension_semantics=("arbitrary", "arbitrary"), vmem_limit_bytes=VMEM_LIMIT),
        name=f"band_attn_g{g}",
    )(qkv, qkv, qkv, qkv, qkv, qkv, qkv)


def _merge_kernel(x_ref, ga_ref, gb_ref, oa_ref, o0_ref, o1_ref, o2_ref, l0_ref, l1_ref, l2_ref,
                  wpa_ref, wpb_ref, wo_ref, gffn_ref, x1_ref, h2_ref):
    heads = []
    for hh in range(H_B):
        l0, l1, l2 = l0_ref[hh], l1_ref[hh], l2_ref[hh]
        m = jnp.maximum(jnp.maximum(l0, l1), l2)
        w0, w1, w2 = jnp.exp(l0 - m), jnp.exp(l1 - m), jnp.exp(l2 - m)
        heads.append(((w0 * o0_ref[hh] + w1 * o1_ref[hh] + w2 * o2_ref[hh]) / (w0 + w1 + w2)).astype(BF16))
    ob = jnp.concatenate(heads, axis=1)
    pa = jnp.dot(oa_ref[...], wpa_ref[...], preferred_element_type=F32)
    pb = jnp.dot(ob, wpb_ref[...], preferred_element_type=F32)
    sigmoid = lambda g: 0.5 * jnp.tanh(0.5 * g.astype(F32)) + 0.5
    merged = sigmoid(ga_ref[...]) * pa + sigmoid(gb_ref[...]) * pb
    x1 = x_ref[...] + jnp.dot(merged.astype(BF16), wo_ref[...], preferred_element_type=F32)
    x1_ref[...] = x1
    ms = jnp.mean(x1 * x1, axis=-1, keepdims=True)
    h2_ref[...] = (x1 * lax.rsqrt(ms + NORM_EPS) * gffn_ref[...]).astype(BF16)


def _merge(x2, z2, oa2, obs, lses, wpa, wpb, wo, gffn, seq, tm):
    T = x2.shape[0]
    tiles_per_seq = seq // tm
    row = lambda w: pl.BlockSpec((tm, w), lambda i: (i, 0))
    per_head = pl.BlockSpec((None, H_B, tm, HD_B), lambda i: (i // tiles_per_seq, 0, i % tiles_per_seq, 0))
    const = lambda a: pl.BlockSpec(a.shape, lambda i: (0, 0), pipeline_mode=pl.Buffered(1))
    return pl.pallas_call(
        _merge_kernel,
        grid=(T // tm,),
        in_specs=[row(D_MODEL),
                  pl.BlockSpec((tm, D_MODEL), lambda i: (i, COL_GATE_A // D_MODEL)),
                  pl.BlockSpec((tm, D_MODEL), lambda i: (i, COL_GATE_B // D_MODEL)),
                  row(OA_W)] + [per_head] * 6 + [const(wpa), const(wpb), const(wo), const(gffn)],
        out_specs=[row(D_MODEL), row(D_MODEL)],
        out_shape=[jax.ShapeDtypeStruct((T, D_MODEL), F32), jax.ShapeDtypeStruct((T, D_MODEL), BF16)],
        compiler_params=pltpu.CompilerParams(
            dimension_semantics=("parallel",), vmem_limit_bytes=VMEM_LIMIT),
        name="merge",
    )(x2, z2, z2, oa2, *obs, *lses, wpa, wpb, wo, gffn)


HALO = 16
FFN_ROW_SPLIT = 2


def _ffn_kernel(h_ref, hp_ref, hn_ref, x1_ref, wg_ref, wv_ref, cwg_ref, cwv_ref, cbg_ref, cbv_ref,
                wd_ref, o_ref, hcat_scr, acc_scr, *, tm, tiles_per_seq, n_f):
    i = pl.program_id(0)
    f = pl.program_id(1)

    @pl.when(f == 0)
    def _():
        pos = i % tiles_per_seq
        hp = hp_ref[...]
        hn = hn_ref[...]
        hcat_scr[0:HALO, :] = jnp.where(pos == 0, jnp.zeros_like(hp), hp)
        hcat_scr[HALO:HALO + tm, :] = h_ref[...]
        hcat_scr[HALO + tm:, :] = jnp.where(pos == tiles_per_seq - 1, jnp.zeros_like(hn), hn)
        acc_scr[...] = jnp.zeros(acc_scr.shape, F32)

    rows = tm + 2 * HALO
    cut = rows // FFN_ROW_SPLIT

    def up(w_ref):
        return jnp.concatenate([jnp.dot(hcat_scr[r * cut:(r + 1) * cut, :], w_ref[...], preferred_element_type=F32)
                                for r in range(FFN_ROW_SPLIT)], axis=0)

    def conv(a, cw_ref, cb_ref):
        a_prev = pltpu.roll(a, 1, 0)[HALO:HALO + tm]
        a_next = pltpu.roll(a, rows - 1, 0)[HALO:HALO + tm]
        cw = cw_ref[...]
        return a_prev * cw[0:1] + a[HALO:HALO + tm] * cw[1:2] + a_next * cw[2:3] + cb_ref[...]

    ug = conv(up(wg_ref), cwg_ref, cbg_ref)
    uv = conv(up(wv_ref), cwv_ref, cbv_ref)
    act = (0.5 * ug * (1.0 + lax.erf(ug * np.float32(math.sqrt(0.5)))) * uv).astype(BF16)
    part = tm // FFN_ROW_SPLIT
    for r in range(FFN_ROW_SPLIT):
        acc_scr[r * part:(r + 1) * part, :] += jnp.dot(act[r * part:(r + 1) * part], wd_ref[...],
                                                       preferred_element_type=F32)

    @pl.when(f == n_f - 1)
    def _():
        o_ref[...] = x1_ref[...] + acc_scr[...]


def _ffn(h2, x1, w_up, conv_w, conv_b, w_down, seq, tm, tf):
    T = h2.shape[0]
    n_f = D_FF // tf
    hb = tm // HALO
    n_hb = T // HALO
    kernel = functools.partial(_ffn_kernel, tm=tm, tiles_per_seq=seq // tm, n_f=n_f)
    return pl.pallas_call(
        kernel,
        grid=(T // tm, n_f),
        in_specs=[
            pl.BlockSpec((tm, D_MODEL), lambda i, f: (i, 0)),
            pl.BlockSpec((HALO, D_MODEL), lambda i, f: (jnp.maximum(i * hb - 1, 0), 0)),
            pl.BlockSpec((HALO, D_MODEL), lambda i, f: (jnp.minimum((i + 1) * hb, n_hb - 1), 0)),
            pl.BlockSpec((tm, D_MODEL), lambda i, f: (i, 0)),
            pl.BlockSpec((D_MODEL, tf), lambda i, f: (0, f)),
            pl.BlockSpec((D_MODEL, tf), lambda i, f: (0, f + n_f)),
            pl.BlockSpec((3, tf), lambda i, f: (0, f)),
            pl.BlockSpec((3, tf), lambda i, f: (0, f + n_f)),
            pl.BlockSpec((1, tf), lambda i, f: (0, f)),
            pl.BlockSpec((1, tf), lambda i, f: (0, f + n_f)),
            pl.BlockSpec((tf, D_MODEL), lambda i, f: (f, 0)),
        ],
        out_specs=pl.BlockSpec((tm, D_MODEL), lambda i, f: (i, 0)),
        out_shape=jax.ShapeDtypeStruct((T, D_MODEL), F32),
        scratch_shapes=[pltpu.VMEM((tm + 2 * HALO, D_MODEL), BF16), pltpu.VMEM((tm, D_MODEL), F32)],
        compiler_params=pltpu.CompilerParams(
            dimension_semantics=("parallel", "arbitrary"), vmem_limit_bytes=VMEM_LIMIT),
        name="ffn",
    )(h2, h2, h2, x1, w_up, w_up, conv_w, conv_w, conv_b, conv_b, w_down)


def _prep_params(g_mix_norm, w_in, g_qa, g_ka, lam_q1, lam_k1, lam_q2, lam_k2, g_subln, g_qb, g_kb,
                 w_pa, w_pb, w_o, g_ffn_norm, w_up, conv_w, conv_b, w_down):
    qa, ka, va, qb, kb, vb, gate_a, gate_b = jnp.split(
        w_in, list(np.cumsum([QA_W, QA_W, QA_W, QB_W, QB_W, QB_W, D_MODEL])), axis=-1)
    w_in_p = jnp.concatenate([gate_a, gate_b, qa, ka, va, qb, kb, vb], axis=-1).astype(BF16)
    ones = lambda n: jnp.ones((n,), F32)
    gvec = jnp.concatenate([
        ones(2 * D_MODEL),
        jnp.tile(g_qa.astype(F32), 2 * H_A) * (HD_A ** -0.5 * LOG2E),
        jnp.tile(g_ka.astype(F32), 2 * H_A),
        ones(QA_W),
        jnp.tile(g_qb.astype(F32), N_GROUPS_B * H_B) * (HD_B ** -0.5),
        jnp.tile(g_kb.astype(F32), N_GROUPS_B * H_B),
        ones(QB_W)]).reshape(1, N_IN)
    idx = np.arange(TN)
    gmat = jnp.asarray(np.stack([
        (idx[:, None] // HD_A == idx[None, :] // HD_A) / HD_A,
        (idx[:, None] // HD_B == idx[None, :] // HD_B) / HD_B]).astype(np.float32), BF16)
    lamv = jnp.stack([lam_q1, lam_k1, lam_q2, lam_k2]).astype(F32)
    return dict(
        gmix=g_mix_norm.astype(F32).reshape(1, D_MODEL), w_in=w_in_p, gvec=gvec, gmat=gmat, lamv=lamv,
        gsub=g_subln.astype(F32).reshape(1, 2 * HD_A), wpa=w_pa.astype(BF16), wpb=w_pb.astype(BF16),
        wo=w_o.astype(BF16), gffn=g_ffn_norm.astype(F32).reshape(1, D_MODEL), w_up=w_up.astype(BF16),
        conv_w=conv_w.astype(F32), conv_b=conv_b.astype(F32).reshape(1, 2 * D_FF),
        w_down=w_down.astype(BF16))


def _tiles(seq):
    return dict(in_proj_tm=min(1024, seq), attn_tq=min(512, seq // 2), attn_tk=min(1024, seq // 2),
                merge_tm=256, ffn_tm=512, ffn_tf=512)


def _encoder_layer(x, layer_idx, p):
    B, S, _ = x.shape
    T = B * S
    t = _tiles(S)
    lam_init = 0.8 - 0.6 * math.exp(-0.3 * layer_idx)
    x2 = x.reshape(T, D_MODEL)
    z2, *groups = _in_proj(x2, p["gmix"], p["w_in"], p["gvec"], p["gmat"], batch=B, seq=S,
                           tm=t["in_proj_tm"])
    z3 = z2.reshape(B, S, COL_QB)
    oa = _diff_attn(z3, jnp.asarray(_alibi_slopes(H_A)), p["lamv"], p["gsub"], lam_init,
                    tq=t["attn_tq"], tk=t["attn_tk"])
    obs, lses = [], []
    for g, (window, dil) in enumerate(DIL_CONFIGS):
        o, lse = _band_attn(groups[g], g, window, dil)
        obs.append(o)
        lses.append(lse)
    x1, h2 = _merge(x2, z2, oa.reshape(T, OA_W), obs, lses, p["wpa"], p["wpb"], p["wo"], p["gffn"],
                    seq=S, tm=t["merge_tm"])
    y = _ffn(h2, x1, p["w_up"], p["conv_w"], p["conv_b"], p["w_down"], seq=S, tm=t["ffn_tm"], tf=t["ffn_tf"])
    return y.reshape(B, S, D_MODEL)


def kernel(x_prompt, x_sample, g_mix_norm, w_in, g_qa, g_ka, lam_q1, lam_k1, lam_q2, lam_k2, g_subln,
           g_qb, g_kb, w_pa, w_pb, w_o, g_ffn_norm, w_up, conv_w, conv_b, w_down):
    y_prompt, y_sample = x_prompt, x_sample
    for l in range(w_in.shape[0]):
        p = _prep_params(g_mix_norm[l], w_in[l], g_qa[l], g_ka[l], lam_q1[l], lam_k1[l], lam_q2[l],
                         lam_k2[l], g_subln[l], g_qb[l], g_kb[l], w_pa[l], w_pb[l], w_o[l],
                         g_ffn_norm[l], w_up[l], conv_w[l], conv_b[l], w_down[l])
        y_prompt = _encoder_layer(y_prompt, l, p)
        y_sample = _encoder_layer(y_sample, l, p)
    return (y_prompt, y_sample)
```

```python
import functools
import math

import numpy as np
import jax
import jax.numpy as jnp
from jax import lax
from jax.experimental import pallas as pl
from jax.experimental.pallas import tpu as pltpu

D_MODEL = 2048
H_A = 8
HD_A = 64
DIL_CONFIGS = ((128, 1), (512, 4), (2048, 16))
N_GROUPS_B = 3
H_B = 4
HD_B = 128
D_FF = 5632
Q_BLOCK = 128
NORM_EPS = 1e-6
SUBLN_EPS = 1e-5
NEG_INF = -1e30

QA_W = H_A * 2 * HD_A
QB_W = N_GROUPS_B * H_B * HD_B
OA_W = QA_W
OB_W = H_B * HD_B
N_IN = 3 * QA_W + 3 * QB_W + 2 * D_MODEL

LANES = 128
COL_GATE_A = 0
COL_GATE_B = D_MODEL
COL_QA = 2 * D_MODEL
COL_KA = COL_QA + QA_W
COL_VA = COL_KA + QA_W
COL_QB = COL_VA + QA_W
COL_KB = COL_QB + QB_W
COL_VB = COL_KB + QB_W
TN = 512
N_TILES = N_IN // TN
VMEM_LIMIT = 56 * 1024 * 1024

F32 = jnp.float32
BF16 = jnp.bfloat16


def _alibi_slopes(n):
    return np.asarray(2.0 ** (-8.0 * (np.arange(n) + 1) / n), dtype=np.float32)


T_QA, T_VA, T_QB, T_KB, T_VB = COL_QA // TN, COL_VA // TN, COL_QB // TN, COL_KB // TN, COL_VB // TN
DILS = tuple(d for _, d in DIL_CONFIGS)


def _in_proj_kernel(x_ref, gmix_ref, w_ref, gvec_ref, gmat_ref, z_ref, g0_ref, g1_ref, g2_ref,
                    h_scr, t_scr, *, tm):
    j = pl.program_id(1)

    @pl.when(j == 0)
    def _():
        x = x_ref[...]
        ms = jnp.mean(x * x, axis=-1, keepdims=True)
        h_scr[...] = (x * lax.rsqrt(ms + NORM_EPS) * gmix_ref[...]).astype(BF16)

    def plain():
        return jnp.dot(h_scr[...], w_ref[...], preferred_element_type=F32)

    def normed():
        z = plain()
        ms = jnp.dot((z * z).astype(BF16), gmat_ref[...], preferred_element_type=F32)
        return z * lax.rsqrt(ms + NORM_EPS) * gvec_ref[...]

    @pl.when((j >= T_QA) & (j < T_VA))
    def _():
        z_ref[...] = normed().astype(BF16)

    @pl.when((j < T_QA) | ((j >= T_VA) & (j < T_QB)))
    def _():
        z_ref[...] = plain().astype(BF16)

    def scatter(g_ref, dil, val):
        if dil == 1:
            g_ref[0] = val.astype(BF16)
        else:
            for kb in range(TN // LANES):
                t_scr[kb] = val[:, kb * LANES:(kb + 1) * LANES]
            for c in range(dil):
                for kb in range(TN // LANES):
                    g_ref[c, :, kb * LANES:(kb + 1) * LANES] = (
                        t_scr[kb, pl.ds(c, tm // dil, stride=dil), :].astype(BF16))

    for g, (g_ref, dil) in enumerate(zip((g0_ref, g1_ref, g2_ref), DILS)):
        @pl.when((j == T_QB + g) | (j == T_KB + g))
        def _(g_ref=g_ref, dil=dil):
            scatter(g_ref, dil, normed())

        @pl.when(j == T_VB + g)
        def _(g_ref=g_ref, dil=dil):
            scatter(g_ref, dil, plain())


def _in_proj(x2, gmix, w_in_p, gvec, gmat, batch, seq, tm):
    T = x2.shape[0]
    tiles_per_seq = seq // tm
    main_w = COL_QB

    def group_spec(g, dil):
        def imap(i, j):
            sec = (j >= T_KB + g).astype(jnp.int32) + (j >= T_VB + g).astype(jnp.int32)
            return (sec, i // tiles_per_seq, 0, i % tiles_per_seq, 0)
        return pl.BlockSpec((None, None, dil, tm // dil, TN), imap)

    return pl.pallas_call(
        functools.partial(_in_proj_kernel, tm=tm),
        grid=(T // tm, N_TILES),
        in_specs=[
            pl.BlockSpec((tm, D_MODEL), lambda i, j: (i, 0)),
            pl.BlockSpec((1, D_MODEL), lambda i, j: (0, 0)),
            pl.BlockSpec((D_MODEL, TN), lambda i, j: (0, j)),
            pl.BlockSpec((1, TN), lambda i, j: (0, j)),
            pl.BlockSpec((None, TN, TN), lambda i, j: (jnp.where(j < T_QB, 0, 1), 0, 0)),
        ],
        out_specs=[pl.BlockSpec((tm, TN), lambda i, j: (i, jnp.minimum(j, T_QB - 1)))]
                  + [group_spec(g, dil) for g, dil in enumerate(DILS)],
        out_shape=[jax.ShapeDtypeStruct((T, main_w), BF16)]
                  + [jax.ShapeDtypeStruct((3, batch, dil, seq // dil, TN), BF16) for dil in DILS],
        scratch_shapes=[pltpu.VMEM((tm, D_MODEL), BF16), pltpu.VMEM((TN // LANES, tm, LANES), F32)],
        compiler_params=pltpu.CompilerParams(
            dimension_semantics=("arbitrary", "arbitrary"), vmem_limit_bytes=VMEM_LIMIT),
        name="in_proj",
    )(x2, gmix, w_in_p, gvec, gmat)


LOG2E = math.log2(math.e)
AUG_P, AUG_J, AUG_R = 0, 3, 6
REF_HEADROOM = 60.0
BOUND_LIMIT = 80.0


def _split3(x):
    def top(a):
        bits = lax.bitcast_convert_type(a, jnp.uint32) & jnp.uint32(0xFFFF0000)
        return lax.bitcast_convert_type(bits, F32)
    hi = top(x)
    mid = top(x - hi)
    lo = top(x - hi - mid)
    return hi, mid, lo


def _key_table(tk):
    def top(a):
        return (a.view(np.uint32) & np.uint32(0xFFFF0000)).view(np.float32)
    x = (np.arange(tk, dtype=np.float32) * np.float32(LOG2E)).astype(np.float32)
    hi = top(x)
    mid = top(x - hi)
    lo = top(x - hi - mid)
    tab = np.zeros((tk, LANES), np.float32)
    tab[:, AUG_P:AUG_P + 3] = 1.0
    tab[:, AUG_J], tab[:, AUG_J + 1], tab[:, AUG_J + 2] = hi, mid, lo
    tab[:, AUG_R] = 1.0
    return jnp.asarray(tab).astype(BF16)


def _diff_attn_kernel(slopes_ref, lamv_ref, q_ref, k_ref, v_ref, ktab_ref, gsub_ref, o_ref,
                      kcat, vcat, kstat, acc, m_scr, qpos, own_bias, *, tq, tk, seq, lam_init):
    h = pl.program_id(1)
    i = pl.program_id(2)
    n_chunks = seq // tk
    slope = slopes_ref[h]
    cs = slope * LOG2E
    nt = (((1,), (1,)), ((), ()))

    @pl.when(i == 0)
    def _():
        lane_k = lax.broadcasted_iota(jnp.int32, (tk, LANES), 1)
        ones_col = jnp.ones((tk, LANES), BF16)

        def fill(c, carry):
            n1, n2 = carry
            sl = pl.ds(pl.multiple_of(c * tk, tk), tk)
            kc = k_ref[sl, :]
            kcat[sl, 0:LANES] = kc
            kcat[sl, LANES:2 * LANES] = ktab_ref[...]
            vcat[sl, 0:LANES] = v_ref[sl, :]
            vcat[sl, LANES:2 * LANES] = ones_col
            kf = kc.astype(F32)
            kk = kf * kf
            s1 = jnp.sum(jnp.where(lane_k < HD_A, kk, 0.0), axis=-1, keepdims=True)
            s2 = jnp.sum(jnp.where(lane_k >= HD_A, kk, 0.0), axis=-1, keepdims=True)
            return jnp.maximum(n1, s1), jnp.maximum(n2, s2)

        z = jnp.zeros((tk, 1), F32)
        n1, n2 = lax.fori_loop(0, n_chunks, fill, (z, z))
        kstat[0:1, :] = jnp.broadcast_to(jnp.sqrt(jnp.max(n1, axis=0, keepdims=True)), (1, LANES))
        kstat[1:2, :] = jnp.broadcast_to(jnp.sqrt(jnp.max(n2, axis=0, keepdims=True)), (1, LANES))

        row2 = lax.broadcasted_iota(jnp.int32, (2 * tq, LANES), 0)
        lane2 = lax.broadcasted_iota(jnp.int32, (2 * tq, LANES), 1)
        hi, mid, lo = _split3(cs * jnp.where(row2 >= tq, row2 - tq, row2).astype(F32))
        ppos = jnp.where(lane2 == AUG_P, hi, jnp.where(lane2 == AUG_P + 1, mid,
                                                       jnp.where(lane2 == AUG_P + 2, lo, 0.0)))
        qpos[...] = jnp.where((lane2 >= AUG_J) & (lane2 < AUG_J + 3), slope, 0.0) - ppos
        row = lax.broadcasted_iota(jnp.int32, (2 * tq, tq), 0)
        col = lax.broadcasted_iota(jnp.int32, (2 * tq, tq), 1)
        own_bias[...] = cs * jnp.abs(jnp.where(row >= tq, row - tq, row) - col).astype(F32)

    q = q_ref[...]
    lane = lax.broadcasted_iota(jnp.int32, (tq, LANES), 1)
    qf = q.astype(F32)
    qq = qf * qf
    qn1 = jnp.sqrt(jnp.sum(jnp.where(lane < HD_A, qq, 0.0), axis=-1, keepdims=True))
    qn2 = jnp.sqrt(jnp.sum(jnp.where(lane >= HD_A, qq, 0.0), axis=-1, keepdims=True))
    bound = jnp.concatenate([qn1 * kstat[0:1, 0:1], qn2 * kstat[1:2, 0:1]], axis=0) * 1.01
    bound_ok = jnp.max(bound) <= BOUND_LIMIT

    zero = jnp.zeros_like(q)
    base = jnp.concatenate([jnp.where(lane < HD_A, q, zero), jnp.where(lane >= HD_A, q, zero)], axis=0)
    lane2 = lax.broadcasted_iota(jnp.int32, (2 * tq, LANES), 1)
    rcol = jnp.where(lane2 == AUG_R, REF_HEADROOM - bound, 0.0)
    q_left = jnp.concatenate([base, (rcol + qpos[...]).astype(BF16)], axis=1)
    q_right = jnp.concatenate([base, (rcol - qpos[...]).astype(BF16)], axis=1)
    q_diag = jnp.concatenate([base, rcol.astype(BF16)], axis=1)
    q_plain = jnp.concatenate([base, jnp.zeros_like(base)], axis=1)
    i0 = i * tq
    cd = i0 // tk

    def chunk(c):
        return pl.ds(pl.multiple_of(c * tk, tk), tk)

    def fixed_reference():
        acc[...] = jnp.zeros(acc.shape, F32)

        def folded(keys, left, base):
            off = cs * (i0 - base).astype(F32)
            s = lax.dot_general(jnp.where(left, q_left, q_right), kcat[keys, :], nt,
                                preferred_element_type=F32) + jnp.where(left, -off, off)
            acc[...] += jnp.dot(jnp.exp2(s).astype(BF16), vcat[keys, :], preferred_element_type=F32)

        def body(c, carry):
            cc = c + (c >= cd).astype(jnp.int32)
            folded(chunk(cc), cc < cd, cc * tk)
            return carry

        n_main = n_chunks - 1
        lax.fori_loop(0, n_main, body, 0, unroll=next(u for u in (15, 5, 3, 2, 1) if n_main % u == 0))

        own = (i0 - cd * tk) // tq
        s = lax.dot_general(q_diag, kcat[pl.ds(pl.multiple_of(i0, tq), tq), :], nt, preferred_element_type=F32)
        s = s - own_bias[...]
        acc[...] += jnp.dot(jnp.exp2(s).astype(BF16), vcat[pl.ds(pl.multiple_of(i0, tq), tq), :],
                            preferred_element_type=F32)
        for k in range(1, tk // tq):
            blk = (own + k) % (tk // tq)
            keys = pl.ds(pl.multiple_of(cd * tk + blk * tq, tq), tq)
            folded(keys, blk < own, cd * tk)

    def online_reference():
        acc[...] = jnp.zeros(acc.shape, F32)
        m_scr[...] = jnp.full(m_scr.shape, NEG_INF, F32)
        row = lax.broadcasted_iota(jnp.int32, (2 * tq, tq), 0)
        col = lax.broadcasted_iota(jnp.int32, (2 * tq, tq), 1)
        rel_own = (jnp.where(row >= tq, row - tq, row) - col).astype(F32)

        def body(c, carry):
            keys = pl.ds(pl.multiple_of(c * tq, tq), tq)
            s = lax.dot_general(q_plain, kcat[keys, :], nt, preferred_element_type=F32)
            s = s - cs * jnp.abs(rel_own + (i0 - c * tq).astype(F32))
            m_prev = m_scr[...]
            m_new = jnp.maximum(m_prev, jnp.max(s, axis=-1, keepdims=True))
            p = jnp.exp2(s - m_new)
            acc[...] = jnp.exp2(m_prev - m_new) * acc[...] + jnp.dot(
                p.astype(BF16), vcat[keys, :], preferred_element_type=F32)
            m_scr[...] = m_new
            return carry

        lax.fori_loop(0, seq // tq, body, 0)

    fixed_reference()
    pl.when(jnp.logical_not(bound_ok))(online_reference)

    lamv = lamv_ref[...]
    lam = (jnp.exp(jnp.sum(lamv[0:1] * lamv[1:2], axis=-1, keepdims=True))
           - jnp.exp(jnp.sum(lamv[2:3] * lamv[3:4], axis=-1, keepdims=True)) + lam_init)
    a = acc[...]
    o = a[:, 0:LANES] / a[:, LANES:2 * LANES]
    d = o[:tq] - lam * o[tq:]
    ms = jnp.mean(d * d, axis=-1, keepdims=True)
    o_ref[...] = (d * lax.rsqrt(ms + SUBLN_EPS) * gsub_ref[...] * (1.0 - lam_init)).astype(BF16)


def _diff_attn(z3, slopes_a, lamv, gsub, lam_init, tq, tk):
    B, S, _ = z3.shape
    assert tk % tq == 0 and S % tk == 0
    qb0, kb0, vb0 = COL_QA // LANES, COL_KA // LANES, COL_VA // LANES
    kernel = functools.partial(_diff_attn_kernel, tq=tq, tk=tk, seq=S, lam_init=lam_init)
    return pl.pallas_call(
        kernel,
        grid=(B, H_A, S // tq),
        in_specs=[
            pl.BlockSpec(memory_space=pltpu.SMEM),
            pl.BlockSpec((4, HD_A), lambda b, h, i: (0, 0)),
            pl.BlockSpec((None, tq, LANES), lambda b, h, i: (b, i, qb0 + h)),
            pl.BlockSpec((None, S, LANES), lambda b, h, i: (b, 0, kb0 + h)),
            pl.BlockSpec((None, S, LANES), lambda b, h, i: (b, 0, vb0 + h)),
            pl.BlockSpec((tk, LANES), lambda b, h, i: (0, 0)),
            pl.BlockSpec((1, LANES), lambda b, h, i: (0, 0)),
        ],
        out_specs=pl.BlockSpec((None, tq, LANES), lambda b, h, i: (b, i, h)),
        out_shape=jax.ShapeDtypeStruct((B, S, OA_W), BF16),
        scratch_shapes=[pltpu.VMEM((S, 2 * LANES), BF16), pltpu.VMEM((S, 2 * LANES), BF16),
                        pltpu.VMEM((8, LANES), F32), pltpu.VMEM((2 * tq, 2 * LANES), F32),
                        pltpu.VMEM((2 * tq, 1), F32), pltpu.VMEM((2 * tq, LANES), F32),
                        pltpu.VMEM((2 * tq, tq), F32)],
        compiler_params=pltpu.CompilerParams(
            dimension_semantics=("arbitrary", "arbitrary", "arbitrary"), vmem_limit_bytes=VMEM_LIMIT),
        name="diff_attn",
    )(slopes_a, lamv, z3, z3, z3, _key_table(tk), gsub)


BAND_STEP_BYTES = 2 * 1024 * 1024


def _band_attn_kernel(q_ref, kp_ref, kc_ref, kn_ref, vp_ref, vc_ref, vn_ref, o_ref, lse_ref,
                      *, length, dil, half, slopes, nsub):
    i = pl.program_id(1)
    qb = Q_BLOCK
    kw = qb + 2 * half
    nrow = H_B * qb
    row = lax.broadcasted_iota(jnp.int32, (nrow, kw), 0)
    col = lax.broadcasted_iota(jnp.int32, (nrow, kw), 1)
    rel = col - half - (row & (qb - 1))
    in_band = jnp.abs(rel) <= half
    slope_rows = jnp.full((nrow, kw), float(slopes[0]), F32)
    for hh in range(1, H_B):
        slope_rows = jnp.where(row >= hh * qb, float(slopes[hh]), slope_rows)
    bias = slope_rows * (dil * jnp.abs(rel)).astype(F32)
    valid = []
    for sb in range(nsub):
        kpos = (i * nsub + sb) * qb - half + col
        valid.append(in_band & (kpos >= 0) & (kpos < length))
    lane_head = lax.broadcasted_iota(jnp.int32, (qb, TN), 1) // HD_B
    nt = (((1,), (1,)), ((), ()))

    def window(prev_ref, cur_ref, next_ref, c, sb):
        lo, hi = sb * qb - half, sb * qb + qb + half
        parts = []
        if lo < 0:
            parts.append(prev_ref[c, qb + lo:qb, :])
        parts.append(cur_ref[c, max(lo, 0):min(hi, nsub * qb), :])
        if hi > nsub * qb:
            parts.append(next_ref[c, 0:hi - nsub * qb, :])
        return parts[0] if len(parts) == 1 else jnp.concatenate(parts, axis=0)

    def one_class(c, carry):
        for sb in range(nsub):
            start = c + sb * qb * dil
            rows = pl.ds(start, qb, stride=dil) if dil > 1 else pl.ds(sb * qb, qb)
            q4 = q_ref[c, sb * qb:(sb + 1) * qb, :]
            zero = jnp.zeros_like(q4)
            qbd = jnp.concatenate([jnp.where(lane_head == hh, q4, zero) for hh in range(H_B)], axis=0)
            kk = window(kp_ref, kc_ref, kn_ref, c, sb)
            vv = window(vp_ref, vc_ref, vn_ref, c, sb)
            s = lax.dot_general(qbd, kk, nt, preferred_element_type=F32)
            s = jnp.where(valid[sb], s - bias, NEG_INF)
            m = jnp.max(s, axis=-1, keepdims=True)
            p = jnp.exp(s - m)
            l = jnp.sum(p, axis=-1, keepdims=True)
            o_all = jnp.dot(p.astype(BF16), vv, preferred_element_type=F32)
            lse = m + jnp.log(l)
            for hh in range(H_B):
                rr = slice(hh * qb, (hh + 1) * qb)
                o_ref[hh, rows, :] = o_all[rr, hh * HD_B:(hh + 1) * HD_B] / l[rr]
                lse_ref[hh, rows, :] = jnp.broadcast_to(lse[rr], (qb, HD_B))
        return carry

    lax.fori_loop(0, dil, one_class, 0, unroll=min(dil, max(1, 4 // nsub)))


def _band_attn(qkv, g, window, dil):
    _, B, _, L, _ = qkv.shape
    S = L * dil
    half = (window // 2) // dil
    assert half <= Q_BLOCK and half % 16 == 0
    nb = L // Q_BLOCK
    nsub = max(1, min(4, nb, BAND_STEP_BYTES // (dil * Q_BLOCK * TN * 2)))
    assert nb % nsub == 0
    slopes = _alibi_slopes(N_GROUPS_B * H_B).reshape(N_GROUPS_B, H_B)[g]
    kernel = functools.partial(_band_attn_kernel, length=L, dil=dil, half=half, slopes=slopes, nsub=nsub)

    def own(sec):
        return pl.BlockSpec((None, None, dil, nsub * Q_BLOCK, TN), lambda b, i: (sec, b, 0, i, 0))

    def neighbour(sec, after):
        def imap(b, i):
            blk = (i + 1) * nsub if after else i * nsub - 1
            return (sec, b, 0, jnp.clip(blk, 0, nb - 1), 0)
        return pl.BlockSpec((None, None, dil, Q_BLOCK, TN), imap)

    out_spec = pl.BlockSpec((None, H_B, nsub * Q_BLOCK * dil, HD_B), lambda b, i: (b, 0, i, 0))
    out_shape = jax.ShapeDtypeStruct((B, H_B, S, HD_B), F32)
    return pl.pallas_call(
        kernel,
        grid=(B, nb // nsub),
        in_specs=[own(0), neighbour(1, False), own(1), neighbour(1, True),
                  neighbour(2, False), own(2), neighbour(2, True)],
        out_specs=[out_spec, out_spec],
        out_shape=[out_shape, out_shape],
        compiler_params=pltpu.CompilerParams(
            dimension_semantics=("arbitrary", "arbitrary"), vmem_limit_bytes=VMEM_LIMIT),
        name=f"band_attn_g{g}",
    )(qkv, qkv, qkv, qkv, qkv, qkv, qkv)


def _merge_kernel(x_ref, ga_ref, gb_ref, oa_ref, o0_ref, o1_ref, o2_ref, l0_ref, l1_ref, l2_ref,
                  wpa_ref, wpb_ref, wo_ref, gffn_ref, x1_ref, h2_ref):
    heads = []
    for hh in range(H_B):
        l0, l1, l2 = l0_ref[hh], l1_ref[hh], l2_ref[hh]
        m = jnp.maximum(jnp.maximum(l0, l1), l2)
        w0, w1, w2 = jnp.exp(l0 - m), jnp.exp(l1 - m), jnp.exp(l2 - m)
        heads.append(((w0 * o0_ref[hh] + w1 * o1_ref[hh] + w2 * o2_ref[hh]) / (w0 + w1 + w2)).astype(BF16))
    ob = jnp.concatenate(heads, axis=1)
    pa = jnp.dot(oa_ref[...], wpa_ref[...], preferred_element_type=F32)
    pb = jnp.dot(ob, wpb_ref[...], preferred_element_type=F32)
    sigmoid = lambda g: 0.5 * jnp.tanh(0.5 * g.astype(F32)) + 0.5
    merged = sigmoid(ga_ref[...]) * pa + sigmoid(gb_ref[...]) * pb
    x1 = x_ref[...] + jnp.dot(merged.astype(BF16), wo_ref[...], preferred_element_type=F32)
    x1_ref[...] = x1
    ms = jnp.mean(x1 * x1, axis=-1, keepdims=True)
    h2_ref[...] = (x1 * lax.rsqrt(ms + NORM_EPS) * gffn_ref[...]).astype(BF16)


def _merge(x2, z2, oa2, obs, lses, wpa, wpb, wo, gffn, seq, tm):
    T = x2.shape[0]
    tiles_per_seq = seq // tm
    row = lambda w: pl.BlockSpec((tm, w), lambda i: (i, 0))
    per_head = pl.BlockSpec((None, H_B, tm, HD_B), lambda i: (i // tiles_per_seq, 0, i % tiles_per_seq, 0))
    const = lambda a: pl.BlockSpec(a.shape, lambda i: (0, 0), pipeline_mode=pl.Buffered(1))
    return pl.pallas_call(
        _merge_kernel,
        grid=(T // tm,),
        in_specs=[row(D_MODEL),
                  pl.BlockSpec((tm, D_MODEL), lambda i: (i, COL_GATE_A // D_MODEL)),
                  pl.BlockSpec((tm, D_MODEL), lambda i: (i, COL_GATE_B // D_MODEL)),
                  row(OA_W)] + [per_head] * 6 + [const(wpa), const(wpb), const(wo), const(gffn)],
        out_specs=[row(D_MODEL), row(D_MODEL)],
        out_shape=[jax.ShapeDtypeStruct((T, D_MODEL), F32), jax.ShapeDtypeStruct((T, D_MODEL), BF16)],
        compiler_params=pltpu.CompilerParams(
            dimension_semantics=("parallel",), vmem_limit_bytes=VMEM_LIMIT),
        name="merge",
    )(x2, z2, z2, oa2, *obs, *lses, wpa, wpb, wo, gffn)


HALO = 16


def _ffn_kernel(h_ref, hp_ref, hn_ref, x1_ref, wg_ref, wv_ref, cwg_ref, cwv_ref, cbg_ref, cbv_ref,
                wd_ref, o_ref, hcat_scr, acc_scr, *, tm, tiles_per_seq, n_f):
    i = pl.program_id(0)
    f = pl.program_id(1)

    @pl.when(f == 0)
    def _():
        pos = i % tiles_per_seq
        hp = hp_ref[...]
        hn = hn_ref[...]
        hcat_scr[0:HALO, :] = jnp.where(pos == 0, jnp.zeros_like(hp), hp)
        hcat_scr[HALO:HALO + tm, :] = h_ref[...]
        hcat_scr[HALO + tm:, :] = jnp.where(pos == tiles_per_seq - 1, jnp.zeros_like(hn), hn)
        acc_scr[...] = jnp.zeros(acc_scr.shape, F32)

    hcat = hcat_scr[...]
    rows = tm + 2 * HALO

    def conv_half(w_ref, cw_ref, cb_ref):
        a = jnp.dot(hcat, w_ref[...], preferred_element_type=F32)
        a_prev = pltpu.roll(a, 1, 0)[HALO:HALO + tm]
        a_next = pltpu.roll(a, rows - 1, 0)[HALO:HALO + tm]
        cw = cw_ref[...]
        return a_prev * cw[0:1] + a[HALO:HALO + tm] * cw[1:2] + a_next * cw[2:3] + cb_ref[...]

    ug = conv_half(wg_ref, cwg_ref, cbg_ref)
    uv = conv_half(wv_ref, cwv_ref, cbv_ref)
    act = 0.5 * ug * (1.0 + lax.erf(ug * np.float32(math.sqrt(0.5)))) * uv
    acc_scr[...] += jnp.dot(act.astype(BF16), wd_ref[...], preferred_element_type=F32)

    @pl.when(f == n_f - 1)
    def _():
        o_ref[...] = x1_ref[...] + acc_scr[...]


def _ffn(h2, x1, w_up, conv_w, conv_b, w_down, seq, tm, tf):
    T = h2.shape[0]
    n_f = D_FF // tf
    hb = tm // HALO
    n_hb = T // HALO
    kernel = functools.partial(_ffn_kernel, tm=tm, tiles_per_seq=seq // tm, n_f=n_f)
    return pl.pallas_call(
        kernel,
        grid=(T // tm, n_f),
        in_specs=[
            pl.BlockSpec((tm, D_MODEL), lambda i, f: (i, 0)),
            pl.BlockSpec((HALO, D_MODEL), lambda i, f: (jnp.maximum(i * hb - 1, 0), 0)),
            pl.BlockSpec((HALO, D_MODEL), lambda i, f: (jnp.minimum((i + 1) * hb, n_hb - 1), 0)),
            pl.BlockSpec((tm, D_MODEL), lambda i, f: (i, 0)),
            pl.BlockSpec((D_MODEL, tf), lambda i, f: (0, f)),
            pl.BlockSpec((D_MODEL, tf), lambda i, f: (0, f + n_f)),
            pl.BlockSpec((3, tf), lambda i, f: (0, f)),
            pl.BlockSpec((3, tf), lambda i, f: (0, f + n_f)),
            pl.BlockSpec((1, tf), lambda i, f: (0, f)),
            pl.BlockSpec((1, tf), lambda i, f: (0, f + n_f)),
            pl.BlockSpec((tf, D_MODEL), lambda i, f: (f, 0)),
        ],
        out_specs=pl.BlockSpec((tm, D_MODEL), lambda i, f: (i, 0)),
        out_shape=jax.ShapeDtypeStruct((T, D_MODEL), F32),
        scratch_shapes=[pltpu.VMEM((tm + 2 * HALO, D_MODEL), BF16), pltpu.VMEM((tm, D_MODEL), F32)],
        compiler_params=pltpu.CompilerParams(
            dimension_semantics=("parallel", "arbitrary"), vmem_limit_bytes=VMEM_LIMIT),
        name="ffn",
    )(h2, h2, h2, x1, w_up, w_up, conv_w, conv_w, conv_b, conv_b, w_down)


def _prep_params(g_mix_norm, w_in, g_qa, g_ka, lam_q1, lam_k1, lam_q2, lam_k2, g_subln, g_qb, g_kb,
                 w_pa, w_pb, w_o, g_ffn_norm, w_up, conv_w, conv_b, w_down):
    qa, ka, va, qb, kb, vb, gate_a, gate_b = jnp.split(
        w_in, list(np.cumsum([QA_W, QA_W, QA_W, QB_W, QB_W, QB_W, D_MODEL])), axis=-1)
    w_in_p = jnp.concatenate([gate_a, gate_b, qa, ka, va, qb, kb, vb], axis=-1).astype(BF16)
    ones = lambda n: jnp.ones((n,), F32)
    gvec = jnp.concatenate([
        ones(2 * D_MODEL),
        jnp.tile(g_qa.astype(F32), 2 * H_A) * (HD_A ** -0.5 * LOG2E),
        jnp.tile(g_ka.astype(F32), 2 * H_A),
        ones(QA_W),
        jnp.tile(g_qb.astype(F32), N_GROUPS_B * H_B) * (HD_B ** -0.5),
        jnp.tile(g_kb.astype(F32), N_GROUPS_B * H_B),
        ones(QB_W)]).reshape(1, N_IN)
    idx = np.arange(TN)
    gmat = jnp.asarray(np.stack([
        (idx[:, None] // HD_A == idx[None, :] // HD_A) / HD_A,
        (idx[:, None] // HD_B == idx[None, :] // HD_B) / HD_B]).astype(np.float32), BF16)
    lamv = jnp.stack([lam_q1, lam_k1, lam_q2, lam_k2]).astype(F32)
    return dict(
        gmix=g_mix_norm.astype(F32).reshape(1, D_MODEL), w_in=w_in_p, gvec=gvec, gmat=gmat, lamv=lamv,
        gsub=g_subln.astype(F32).reshape(1, 2 * HD_A), wpa=w_pa.astype(BF16), wpb=w_pb.astype(BF16),
        wo=w_o.astype(BF16), gffn=g_ffn_norm.astype(F32).reshape(1, D_MODEL), w_up=w_up.astype(BF16),
        conv_w=conv_w.astype(F32), conv_b=conv_b.astype(F32).reshape(1, 2 * D_FF),
        w_down=w_down.astype(BF16))


def _tiles(seq):
    return dict(in_proj_tm=min(1024, seq), attn_tq=min(512, seq // 2), attn_tk=min(1024, seq // 2),
                merge_tm=256, ffn_tm=512, ffn_tf=512)


def _encoder_layer(x, layer_idx, p):
    B, S, _ = x.shape
    T = B * S
    t = _tiles(S)
    lam_init = 0.8 - 0.6 * math.exp(-0.3 * layer_idx)
    x2 = x.reshape(T, D_MODEL)
    z2, *groups = _in_proj(x2, p["gmix"], p["w_in"], p["gvec"], p["gmat"], batch=B, seq=S,
                           tm=t["in_proj_tm"])
    z3 = z2.reshape(B, S, COL_QB)
    oa = _diff_attn(z3, jnp.asarray(_alibi_slopes(H_A)), p["lamv"], p["gsub"], lam_init,
                    tq=t["attn_tq"], tk=t["attn_tk"])
    obs, lses = [], []
    for g, (window, dil) in enumerate(DIL_CONFIGS):
        o, lse = _band_attn(groups[g], g, window, dil)
        obs.append(o)
        lses.append(lse)
    x1, h2 = _merge(x2, z2, oa.reshape(T, OA_W), obs, lses, p["wpa"], p["wpb"], p["wo"], p["gffn"],
                    seq=S, tm=t["merge_tm"])
    y = _ffn(h2, x1, p["w_up"], p["conv_w"], p["conv_b"], p["w_down"], seq=S, tm=t["ffn_tm"], tf=t["ffn_tf"])
    return y.reshape(B, S, D_MODEL)


def kernel(x_prompt, x_sample, g_mix_norm, w_in, g_qa, g_ka, lam_q1, lam_k1, lam_q2, lam_k2, g_subln,
           g_qb, g_kb, w_pa, w_pb, w_o, g_ffn_norm, w_up, conv_w, conv_b, w_down):
    y_prompt, y_sample = x_prompt, x_sample
    for l in range(w_in.shape[0]):
        p = _prep_params(g_mix_norm[l], w_in[l], g_qa[l], g_ka[l], lam_q1[l], lam_k1[l], lam_q2[l],
                         lam_k2[l], g_subln[l], g_qb[l], g_kb[l], w_pa[l], w_pb[l], w_o[l],
                         g_ffn_norm[l], w_up[l], conv_w[l], conv_b[l], w_down[l])
        y_prompt = _encoder_layer(y_prompt, l, p)
        y_sample = _encoder_layer(y_sample, l, p)
    return (y_prompt, y_sample)
```
